```python
import math
import jax, jax.numpy as jnp
from jax import lax
import numpy as np

D_MODEL = 2048
BATCH = 16
SEQ = 2048
DEPTH = 2

CTX_LEN = 256
GRID_W = 64
HEAD_DIM = 128
N_Q_HEADS = D_MODEL // 256
N_KV_HEADS = N_Q_HEADS // 4
GQA_GROUP = N_Q_HEADS // N_KV_HEADS
WINDOW = 128
ATTN_BLOCK = 128
AXIS_DIM = HEAD_DIM // 2
ROPE_BASE = 10000.0
Q_W = N_Q_HEADS * HEAD_DIM
KV_W = N_KV_HEADS * HEAD_DIM
GM_GROUPS = 8
GM_W = D_MODEL // 2
GM_GW = GM_W // GM_GROUPS
CHUNK = 128
AB_SPLITS = [Q_W, Q_W + KV_W, Q_W + 2 * KV_W, Q_W + 2 * KV_W + GM_W]
AB_IN_W = Q_W + 2 * KV_W + 2 * GM_W
AB_OUT_W = Q_W + GM_W
POOL_W = D_MODEL
POOL_WINDOWS = (2, 4, 8, 16)
N_POOL = len(POOL_WINDOWS)
POOL_GW = POOL_W // N_POOL
N_EXPERTS = 16
EXPERT_FF = D_MODEL // 2
CAPACITY_FACTOR = 2
N_MOD = 6
LN_EPS = 1e-6
NEG_INF = -1e30

kernel_name = 'hybrid_diffusion_swa_gmlp_pool_ecmoe'


def layer_norm(x, g, b):
    xf = x.astype(jnp.float32)
    mu = jnp.mean(xf, axis=-1, keepdims=True)
    var = jnp.mean(jnp.square(xf - mu), axis=-1, keepdims=True)
    y = (xf - mu) * lax.rsqrt(var + LN_EPS) * g.astype(jnp.float32) + b.astype(jnp.float32)
    return y.astype(x.dtype)


def rope_2d_tables(n_tokens, dtype):
    rows = n_tokens // GRID_W
    row = jnp.repeat(jnp.arange(rows, dtype=jnp.float32), GRID_W)
    col = jnp.tile(jnp.arange(GRID_W, dtype=jnp.float32), rows)
    inv = ROPE_BASE ** (-jnp.arange(0, AXIS_DIM, 2, dtype=jnp.float32) / AXIS_DIM)
    ang_r = row[:, None] * inv
    ang_c = col[:, None] * inv
    return tuple(t[:, None, :].astype(dtype) for t in
                 (jnp.cos(ang_r), jnp.sin(ang_r), jnp.cos(ang_c), jnp.sin(ang_c)))


def _rotate(x, cos, sin):
    x1, x2 = jnp.split(x, 2, axis=-1)
    return jnp.concatenate([x1 * cos - x2 * sin, x2 * cos + x1 * sin], axis=-1)


def apply_rope_2d(x, tabs):
    cos_r, sin_r, cos_c, sin_c = tabs
    return jnp.concatenate([_rotate(x[..., :AXIS_DIM], cos_r, sin_r),
                            _rotate(x[..., AXIS_DIM:], cos_c, sin_c)], axis=-1)


def windowed_gqa(q, k, v, kc, vc, sink):
    B, L, _, dh = q.shape
    nb = L // ATTN_BLOCK
    scale = dh ** -0.5
    qb = q.reshape(B, nb, ATTN_BLOCK, N_KV_HEADS, GQA_GROUP, dh)

    def band(t):
        tp = jnp.pad(t, ((0, 0), (ATTN_BLOCK, ATTN_BLOCK), (0, 0), (0, 0)))
        tp = tp.reshape(B, nb + 2, ATTN_BLOCK, N_KV_HEADS, dh)
        return jnp.concatenate([tp[:, :-2], tp[:, 1:-1], tp[:, 2:]], axis=2)

    kw, vw = band(k), band(v)
    s_loc = jnp.einsum('bnqhgd,bnkhd->bnhgqk', qb, kw).astype(jnp.float32) * scale
    s_ctx = jnp.einsum('bnqhgd,bchd->bnhgqc', qb, kc).astype(jnp.float32) * scale
    blk = jnp.arange(nb)[:, None, None]
    qpos = blk * ATTN_BLOCK + jnp.arange(ATTN_BLOCK)[None, :, None]
    kpos = (blk - 1) * ATTN_BLOCK + jnp.arange(3 * ATTN_BLOCK)[None, None, :]
    valid = (jnp.abs(kpos - qpos) <= WINDOW) & (kpos >= 0) & (kpos < L)
    s_loc = jnp.where(valid[None, :, None, None], s_loc, NEG_INF)
    sink_l = jnp.broadcast_to(sink.astype(jnp.float32).reshape(1, 1, N_KV_HEADS, GQA_GROUP, 1, 1),
                              s_loc.shape[:-1] + (1,))
    p = jax.nn.softmax(jnp.concatenate([s_loc, s_ctx, sink_l], axis=-1), axis=-1).astype(v.dtype)
    n_loc = 3 * ATTN_BLOCK
    n_ctx = kc.shape[1]
    o = (jnp.einsum('bnhgqk,bnkhd->bnqhgd', p[..., :n_loc], vw)
         + jnp.einsum('bnhgqc,bchd->bnqhgd', p[..., n_loc:n_loc + n_ctx], vc))
    return o.reshape(B, L, N_Q_HEADS * dh)


def context_attention(q, k, v, sink):
    B, Lc, _, dh = q.shape
    qg = q.reshape(B, Lc, N_KV_HEADS, GQA_GROUP, dh)
    s = jnp.einsum('bqhgd,bkhd->bhgqk', qg, k).astype(jnp.float32) * dh ** -0.5
    sink_l = jnp.broadcast_to(sink.astype(jnp.float32).reshape(1, N_KV_HEADS, GQA_GROUP, 1, 1),
                              s.shape[:-1] + (1,))
    p = jax.nn.softmax(jnp.concatenate([s, sink_l], axis=-1), axis=-1)[..., :-1].astype(v.dtype)
    return jnp.einsum('bhgqk,bkhd->bqhgd', p, v).reshape(B, Lc, N_Q_HEADS * dh)


def chunk_spatial_gate(u, v, w_s, b_s, g, b):
    B, L, _ = u.shape
    u = jax.nn.gelu(u, approximate=False)
    v = layer_norm(jax.nn.gelu(v, approximate=False), g, b)
    vc = v.reshape(B, L // CHUNK, CHUNK, GM_GROUPS, GM_GW)
    mixed = jnp.einsum('gts,bnsgc->bntgc', w_s, vc) + b_s.T[None, None, :, :, None]
    return u * mixed.reshape(B, L, GM_W)


def multiscale_pool(h, w_pool, pool_scale):
    B, L, _ = h.shape
    hg = h.reshape(B, L, N_POOL, POOL_GW)
    t = jnp.arange(L)
    outs = []
    for gi, w in enumerate(POOL_WINDOWS):
        xg = hg[:, :, gi].astype(jnp.float32)
        cs = jnp.concatenate([jnp.zeros((B, 1, POOL_GW), jnp.float32), jnp.cumsum(xg, axis=1)], axis=1)
        lo = jnp.maximum(t - w // 2, 0)
        hi = jnp.minimum(t + w // 2, L)
        mean = (cs[:, hi] - cs[:, lo]) / (hi - lo).astype(jnp.float32)[None, :, None]
        outs.append((mean - xg).astype(h.dtype) @ w_pool[gi])
    return jnp.concatenate(outs, axis=-1) * pool_scale


def expert_choice_ffn(h, w_router, w_gate, w_up, w_down):
    B, L, Dm = h.shape
    cap = (CAPACITY_FACTOR * L) // N_EXPERTS
    aff = jax.nn.softmax(jnp.einsum('bld,de->ble', h, w_router).astype(jnp.float32), axis=-1)
    gates, idx = lax.top_k(jnp.swapaxes(aff, 1, 2), cap)
    xs = jax.vmap(lambda hb, ib: hb[ib])(h, idx)
    a = jnp.einsum('becd,edf->becf', xs, w_gate)
    u = jnp.einsum('becd,edf->becf', xs, w_up)
    y = jnp.einsum('becf,efd->becd', jax.nn.silu(a) * u, w_down) * gates[..., None].astype(h.dtype)
    return jax.vmap(lambda ib, yb: jnp.zeros((L, Dm), h.dtype).at[ib.reshape(-1)].add(yb.reshape(-1, Dm)))(idx, y)


def _ctx_read_after(i):
    return any(j % 2 == 0 for j in range(i + 1, DEPTH))


def setup_inputs(seed: int = 0) -> dict:
    key = jax.random.key(seed)
    ks = jax.random.split(key, 24)
    n_even = (DEPTH + 1) // 2
    n_odd = DEPTH // 2
    beta = (8.0 * DEPTH) ** -0.25

    def nrm(k, shape, s):
        return jax.random.normal(k, shape, jnp.float32) * s

    return dict(
        x=nrm(ks[0], (BATCH, SEQ, D_MODEL), 1.0),
        c=nrm(ks[1], (BATCH, D_MODEL), 1.0),
        ctx=nrm(ks[2], (BATCH, CTX_LEN, D_MODEL), 1.0),
        c_ctx=nrm(ks[3], (D_MODEL,), 1.0),
        w_mod=nrm(ks[4], (DEPTH, D_MODEL, N_MOD * D_MODEL), 0.5 * D_MODEL ** -0.5),
        b_mod=nrm(ks[5], (DEPTH, N_MOD * D_MODEL), 0.02),
        ln_g=1.0 + nrm(ks[6], (DEPTH, 2, D_MODEL), 0.02),
        ln_b=nrm(ks[7], (DEPTH, 2, D_MODEL), 0.02),
        w_in_ab=nrm(ks[8], (n_even, D_MODEL, AB_IN_W), D_MODEL ** -0.5),
        attn_sink=nrm(ks[9], (n_even, N_Q_HEADS), 0.5),
        w_spatial=nrm(ks[10], (n_even, GM_GROUPS, CHUNK, CHUNK), CHUNK ** -0.5),
        b_spatial=1.0 + nrm(ks[11], (n_even, GM_GROUPS, CHUNK), 0.02),
        gm_ln_g=1.0 + nrm(ks[12], (n_even, GM_W), 0.02),
        gm_ln_b=nrm(ks[13], (n_even, GM_W), 0.02),
        w_out_ab=nrm(ks[14], (n_even, AB_OUT_W, D_MODEL), beta * AB_OUT_W ** -0.5),
        w_in_pool=nrm(ks[15], (n_odd, D_MODEL, POOL_W), D_MODEL ** -0.5),
        w_pool=nrm(ks[16], (n_odd, N_POOL, POOL_GW, POOL_GW), POOL_GW ** -0.5),
        pool_scale=1.0 + nrm(ks[17], (n_odd, POOL_W), 0.02),
        w_out_pool=nrm(ks[18], (n_odd, POOL_W, D_MODEL), beta * POOL_W ** -0.5),
        w_router=nrm(ks[19], (DEPTH, D_MODEL, N_EXPERTS), D_MODEL ** -0.5),
        w_gate=nrm(ks[20], (DEPTH, N_EXPERTS, D_MODEL, EXPERT_FF), D_MODEL ** -0.5),
        w_up=nrm(ks[21], (DEPTH, N_EXPERTS, D_MODEL, EXPERT_FF), D_MODEL ** -0.5),
        w_down=nrm(ks[22], (DEPTH, N_EXPERTS, EXPERT_FF, D_MODEL), beta * EXPERT_FF ** -0.5),
    )


def reference(x, c, ctx, c_ctx, w_mod, b_mod, ln_g, ln_b, w_in_ab, attn_sink, w_spatial, b_spatial,
              gm_ln_g, gm_ln_b, w_out_ab, w_in_pool, w_pool, pool_scale, w_out_pool,
              w_router, w_gate, w_up, w_down):
    B, L, _ = x.shape
    Lc = ctx.shape[1]
    alpha = (2.0 * DEPTH) ** 0.25
    rope = rope_2d_tables(L, x.dtype)
    silu_c = jax.nn.silu(c)
    silu_cc = jax.nn.silu(c_ctx)
    ctx_stream = ctx
    for i in range(DEPTH):
        full_ctx = (ctx_stream is not None) and _ctx_read_after(i)
        mod = jnp.split((silu_c @ w_mod[i] + b_mod[i])[:, None, :], N_MOD, axis=-1)
        if ctx_stream is not None:
            modc = jnp.split(silu_cc @ w_mod[i] + b_mod[i], N_MOD, axis=-1)
            hc = ctx_stream * (1 + modc[1]) + modc[0]
        h = x * (1 + mod[1]) + mod[0]

        if i % 2 == 0:
            e = i // 2
            w_in = w_in_ab[e]
            q, k, v, u, vg = jnp.split(h @ w_in, AB_SPLITS, axis=-1)
            if full_ctx:
                qc, kc, vc, uc, vgc = jnp.split(hc @ w_in, AB_SPLITS, axis=-1)
            else:
                kc, vc = jnp.split(hc @ w_in[:, Q_W:Q_W + 2 * KV_W], 2, axis=-1)
            kc = kc.reshape(B, Lc, N_KV_HEADS, HEAD_DIM)
            vc = vc.reshape(B, Lc, N_KV_HEADS, HEAD_DIM)
            q = apply_rope_2d(q.reshape(B, L, N_Q_HEADS, HEAD_DIM), rope)
            k = apply_rope_2d(k.reshape(B, L, N_KV_HEADS, HEAD_DIM), rope)
            v = v.reshape(B, L, N_KV_HEADS, HEAD_DIM)
            att = windowed_gqa(q, k, v, kc, vc, attn_sink[e])
            gm = chunk_spatial_gate(u, vg, w_spatial[e], b_spatial[e], gm_ln_g[e], gm_ln_b[e])
            y = jnp.concatenate([att, gm], axis=-1) @ w_out_ab[e]
            if full_ctx:
                attc = context_attention(qc.reshape(B, Lc, N_Q_HEADS, HEAD_DIM), kc, vc, attn_sink[e])
                gmc = chunk_spatial_gate(uc, vgc, w_spatial[e], b_spatial[e], gm_ln_g[e], gm_ln_b[e])
                yc = jnp.concatenate([attc, gmc], axis=-1) @ w_out_ab[e]
        else:
            o = i // 2
            y = multiscale_pool(h @ w_in_pool[o], w_pool[o], pool_scale[o]) @ w_out_pool[o]
            if full_ctx:
                yc = multiscale_pool(hc @ w_in_pool[o], w_pool[o], pool_scale[o]) @ w_out_pool[o]
        x = layer_norm(alpha * x + mod[2] * y, ln_g[i, 0], ln_b[i, 0])
        if full_ctx:
            ctx_stream = layer_norm(alpha * ctx_stream + modc[2] * yc, ln_g[i, 0], ln_b[i, 0])

        h = x * (1 + mod[4]) + mod[3]
        x = layer_norm(alpha * x + mod[5] * expert_choice_ffn(h, w_router[i], w_gate[i], w_up[i], w_down[i]),
                       ln_g[i, 1], ln_b[i, 1])
        if full_ctx:
            hc = ctx_stream * (1 + modc[4]) + modc[3]
            ctx_stream = layer_norm(
                alpha * ctx_stream + modc[5] * expert_choice_ffn(hc, w_router[i], w_gate[i], w_up[i], w_down[i]),
                ln_g[i, 1], ln_b[i, 1])
        else:
            ctx_stream = None
    return x
```

```python
import functools

import jax
import jax.numpy as jnp
from jax import lax
from jax.experimental import pallas as pl
from jax.experimental.pallas import tpu as pltpu

F32 = jnp.float32
BF16 = jnp.bfloat16

GRID_W = 64
HEAD_DIM = 128
N_KV_HEADS = 2
GQA_GROUP = 4
N_Q_HEADS = N_KV_HEADS * GQA_GROUP
ATTN_BLOCK = 128
AXIS_DIM = HEAD_DIM // 2
ROPE_BASE = 10000.0
GM_GROUPS = 8
CHUNK = 128
POOL_WINDOWS = (2, 4, 8, 16)
N_EXPERTS = 16
CAPACITY_FACTOR = 2
N_MOD = 6
LN_EPS = 1e-6
NEG_INF = -1e30
INV_SQRT2 = 0.7071067811865476

V7X_LANES = 128
V7X_VMEM_LIMIT_BYTES = 56 * 1024 * 1024

MOD_ROWS = 32
ROUTER_PAD = V7X_LANES


def _cparams(sem):
    return pltpu.CompilerParams(dimension_semantics=sem, vmem_limit_bytes=V7X_VMEM_LIMIT_BYTES)


def _gelu(a):
    return 0.5 * a * (1.0 + lax.erf(a * INV_SQRT2))


def _layer_norm_rows(z, g, b):
    mu = jnp.mean(z, axis=-1, keepdims=True)
    zc = z - mu
    var = jnp.mean(zc * zc, axis=-1, keepdims=True)
    return zc * lax.rsqrt(var + LN_EPS) * g + b


def _mod_kernel(cs_ref, w_ref, b_ref, o_ref):
    cs = cs_ref[...]
    s = cs * jax.nn.sigmoid(cs)
    o_ref[0] = jnp.dot(s.astype(BF16), w_ref[0].astype(BF16), preferred_element_type=F32) + b_ref[0]


def _mod_vectors(cs, w_mod, b_mod):
    depth, d, n = w_mod.shape
    tn = 512
    return pl.pallas_call(
        _mod_kernel,
        grid=(depth, n // tn),
        in_specs=[
            pl.BlockSpec((MOD_ROWS, d), lambda i, j: (0, 0)),
            pl.BlockSpec((1, d, tn), lambda i, j: (i, 0, j)),
            pl.BlockSpec((1, 1, tn), lambda i, j: (i, 0, j)),
        ],
        out_specs=pl.BlockSpec((1, MOD_ROWS, tn), lambda i, j: (i, 0, j)),
        out_shape=jax.ShapeDtypeStruct((depth, MOD_ROWS, n), F32),
        compiler_params=_cparams(("arbitrary", "arbitrary")),
        name="mod_vectors",
    )(cs, w_mod, b_mod.reshape(depth, 1, n))


def _modmm_kernel(x_ref, sh_ref, sc_ref, w_ref, o_ref, *, n_chunk):
    h = (x_ref[0] * (1.0 + sc_ref[0]) + sh_ref[0]).astype(BF16)
    n = w_ref.shape[1]
    for c in range(n // n_chunk):
        sl = slice(c * n_chunk, (c + 1) * n_chunk)
        o_ref[0, :, sl] = jnp.dot(h, w_ref[:, sl], preferred_element_type=F32).astype(o_ref.dtype)


def _modulated_matmul(x, shift, scale, w, *, tm, per_batch_mod):
    b, l, d = x.shape
    n = w.shape[1]
    mod_map = (lambda i, j: (i, 0, 0)) if per_batch_mod else (lambda i, j: (0, 0, 0))
    return pl.pallas_call(
        functools.partial(_modmm_kernel, n_chunk=min(n, 512)),
        grid=(b, l // tm),
        in_specs=[
            pl.BlockSpec((1, tm, d), lambda i, j: (i, j, 0)),
            pl.BlockSpec((1, 1, d), mod_map),
            pl.BlockSpec((1, 1, d), mod_map),
            pl.BlockSpec((d, n), lambda i, j: (0, 0)),
        ],
        out_specs=pl.BlockSpec((1, tm, n), lambda i, j: (i, j, 0)),
        out_shape=jax.ShapeDtypeStruct((b, l, n), BF16),
        compiler_params=_cparams(("arbitrary", "arbitrary")),
        name="modulated_matmul",
    )(x, shift, scale, w)


def _inproj_ab_kernel(x_ref, sh_ref, sc_ref, w_ref, cos_ref, sin_ref,
                      q_ref, k_ref, v_ref, gu_ref, gv_ref, *, q_scale):
    h = (x_ref[0] * (1.0 + sc_ref[0]) + sh_ref[0]).astype(BF16)
    tm = h.shape[0]
    cos = cos_ref[...]
    sin = sin_ref[...]
    lane = lax.broadcasted_iota(jnp.int32, (tm, HEAD_DIM), 1)
    low_half = (lane & (AXIS_DIM // 2)) == 0

    def rope(a):
        partner = jnp.where(low_half, pltpu.roll(a, HEAD_DIM - AXIS_DIM // 2, 1), pltpu.roll(a, AXIS_DIM // 2, 1))
        return a * cos + partner * sin

    nc = 512
    q_w = q_ref.shape[2]
    kv_w = k_ref.shape[2]
    gm_w = gu_ref.shape[2]
    for c in range(w_ref.shape[1] // nc):
        acc = jnp.dot(h, w_ref[:, c * nc:(c + 1) * nc], preferred_element_type=F32)
        base = c * nc
        if base < q_w:
            for j in range(nc // HEAD_DIM):
                a = acc[:, j * HEAD_DIM:(j + 1) * HEAD_DIM]
                q_ref[0, :, base + j * HEAD_DIM: base + (j + 1) * HEAD_DIM] = (rope(a) * q_scale).astype(BF16)
        elif base < q_w + 2 * kv_w:
            for j in range(kv_w // HEAD_DIM):
                a = acc[:, j * HEAD_DIM:(j + 1) * HEAD_DIM]
                k_ref[0, :, j * HEAD_DIM:(j + 1) * HEAD_DIM] = rope(a).astype(BF16)
            v_ref[0] = acc[:, kv_w:2 * kv_w].astype(BF16)
        elif base < q_w + 2 * kv_w + gm_w:
            o = base - (q_w + 2 * kv_w)
            gu_ref[0, :, o:o + nc] = _gelu(acc).astype(BF16)
        else:
            o = base - (q_w + 2 * kv_w + gm_w)
            gv_ref[0, :, o:o + nc] = _gelu(acc).astype(BF16)


def _inproj_ab(x, shift, scale, w, cos_t, sin_t, *, tm):
    b, l, d = x.shape
    q_w = N_Q_HEADS * HEAD_DIM
    kv_w = N_KV_HEADS * HEAD_DIM
    gm_w = d // 2
    n = w.shape[1]
    assert n == q_w + 2 * kv_w + 2 * gm_w and 2 * kv_w == 512
    row = lambda i, j: (i, j, 0)
    return pl.pallas_call(
        functools.partial(_inproj_ab_kernel, q_scale=HEAD_DIM ** -0.5),
        grid=(b, l // tm),
        in_specs=[
            pl.BlockSpec((1, tm, d), row),
            pl.BlockSpec((1, 1, d), lambda i, j: (i, 0, 0)),
            pl.BlockSpec((1, 1, d), lambda i, j: (i, 0, 0)),
            pl.BlockSpec((d, n), lambda i, j: (0, 0)),
            pl.BlockSpec((tm, HEAD_DIM), lambda i, j: (j, 0)),
            pl.BlockSpec((tm, HEAD_DIM), lambda i, j: (j, 0)),
        ],
        out_specs=[
            pl.BlockSpec((1, tm, q_w), row),
            pl.BlockSpec((1, tm, kv_w), row),
            pl.BlockSpec((1, tm, kv_w), row),
            pl.BlockSpec((1, tm, gm_w), row),
            pl.BlockSpec((1, tm, gm_w), row),
        ],
        out_shape=[
            jax.ShapeDtypeStruct((b, l, q_w), BF16),
            jax.ShapeDtypeStruct((b, l, kv_w), BF16),
            jax.ShapeDtypeStruct((b, l, kv_w), BF16),
            jax.ShapeDtypeStruct((b, l, gm_w), BF16),
            jax.ShapeDtypeStruct((b, l, gm_w), BF16),
        ],
        compiler_params=_cparams(("arbitrary", "arbitrary")),
        name="inproj_ab",
    )(x, shift, scale, w, cos_t, sin_t)


def _attn_gmlp_kernel(sink_ref, q_ref, kp_ref, kc_ref, kn_ref, vp_ref, vc_ref, vn_ref, kvx_ref,
                      gu_ref, gv_ref, ws_ref, bs_ref, lng_ref, lnb_ref, o_ref, *, n_blocks):
    n = pl.program_id(1)
    blk = ATTN_BLOCK
    n_ctx = kvx_ref.shape[1]
    rows = GQA_GROUP * blk
    n_keys = 3 * blk + n_ctx
    kv_w = N_KV_HEADS * HEAD_DIM

    qi = lax.broadcasted_iota(jnp.int32, (rows, n_keys), 0) & (blk - 1)
    col = lax.broadcasted_iota(jnp.int32, (rows, n_keys), 1)
    off_prev = jnp.where(n > 0, 0, 1 << 20)
    off_next = jnp.where(n < n_blocks - 1, 0, 1 << 20)
    in_next = (col >= 2 * blk) & (col < 3 * blk)
    valid = ((col >= blk) | (col >= qi + off_prev)) & (jnp.logical_not(in_next) | (col - 2 * blk + off_next <= qi))
    head_of_row = lax.broadcasted_iota(jnp.int32, (rows, 1), 0) // blk

    for hk in range(N_KV_HEADS):
        hs = slice(hk * HEAD_DIM, (hk + 1) * HEAD_DIM)
        qs = jnp.concatenate(
            [q_ref[0, :, (hk * GQA_GROUP + g) * HEAD_DIM:(hk * GQA_GROUP + g + 1) * HEAD_DIM] for g in range(GQA_GROUP)],
            axis=0)
        k_all = jnp.concatenate([kp_ref[0, :, hs], kc_ref[0, :, hs], kn_ref[0, :, hs], kvx_ref[0, :, hs]], axis=0)
        v_all = jnp.concatenate([vp_ref[0, :, hs], vc_ref[0, :, hs], vn_ref[0, :, hs],
                                 kvx_ref[0, :, kv_w + hk * HEAD_DIM: kv_w + (hk + 1) * HEAD_DIM]], axis=0)
        s = lax.dot_general(qs, k_all, (((1,), (1,)), ((), ())), preferred_element_type=F32)
        s = jnp.where(valid, s, NEG_INF)
        sink = jnp.zeros((rows, 1), F32)
        for g in range(GQA_GROUP):
            sink = jnp.where(head_of_row == g, sink_ref[hk * GQA_GROUP + g], sink)
        m = jnp.maximum(jnp.max(s, axis=-1, keepdims=True), sink)
        p = jnp.exp(s - m)
        denom = jnp.sum(p, axis=-1, keepdims=True) + jnp.exp(sink - m)
        o = jnp.dot(p.astype(BF16), v_all, preferred_element_type=F32) / denom
        for g in range(GQA_GROUP):
            h = hk * GQA_GROUP + g
            o_ref[0, :, h * HEAD_DIM:(h + 1) * HEAD_DIM] = o[g * blk:(g + 1) * blk].astype(BF16)

    att_w = N_Q_HEADS * HEAD_DIM
    vn = _layer_norm_rows(gv_ref[0].astype(F32), lng_ref[...], lnb_ref[...]).astype(BF16)
    gw = vn.shape[1] // GM_GROUPS
    for g in range(GM_GROUPS):
        gs = slice(g * gw, (g + 1) * gw)
        mixed = jnp.dot(ws_ref[g], vn[:, gs], preferred_element_type=F32) + bs_ref[:, gs]
        o_ref[0, :, att_w + g * gw: att_w + (g + 1) * gw] = (gu_ref[0, :, gs].astype(F32) * mixed).astype(BF16)


def _attn_gmlp(sink, q, k, v, kvx, gu, gv, ws, bs_full, lng, lnb):
    b, l, q_w = q.shape
    kv_w = k.shape[2]
    gm_w = gu.shape[2]
    nb = l // ATTN_BLOCK
    n_ctx = kvx.shape[1]
    cur = lambda i, j: (i, j, 0)
    prev = lambda i, j: (i, jnp.maximum(j - 1, 0), 0)
    nxt = lambda i, j: (i, jnp.minimum(j + 1, nb - 1), 0)
    const2 = lambda i, j: (0, 0)
    kv_spec = lambda m: pl.BlockSpec((1, ATTN_BLOCK, kv_w), m)
    return pl.pallas_call(
        functools.partial(_attn_gmlp_kernel, n_blocks=nb),
        grid=(b, nb),
        in_specs=[
            pl.BlockSpec(memory_space=pltpu.SMEM),
            pl.BlockSpec((1, ATTN_BLOCK, q_w), cur),
            kv_spec(prev), kv_spec(cur), kv_spec(nxt),
            kv_spec(prev), kv_spec(cur), kv_spec(nxt),
            pl.BlockSpec((1, n_ctx, 2 * kv_w), lambda i, j: (i, 0, 0)),
            pl.BlockSpec((1, ATTN_BLOCK, gm_w), cur),
            pl.BlockSpec((1, ATTN_BLOCK, gm_w), cur),
            pl.BlockSpec((GM_GROUPS, CHUNK, CHUNK), lambda i, j: (0, 0, 0)),
            pl.BlockSpec((CHUNK, gm_w), const2),
            pl.BlockSpec((1, gm_w), const2),
            pl.BlockSpec((1, gm_w), const2),
        ],
        out_specs=pl.BlockSpec((1, ATTN_BLOCK, q_w + gm_w), cur),
        out_shape=jax.ShapeDtypeStruct((b, l, q_w + gm_w), BF16),
        compiler_params=_cparams(("arbitrary", "arbitrary")),
        name="attn_gmlp",
    )(sink, q, k, k, k, v, v, v, kvx, gu, gv, ws, bs_full, lng, lnb)


def _pool_kernel(hp_ref, hc_ref, hn_ref, w_ref, sc_ref, o_ref, *, seq_len):
    n = pl.program_id(1)
    blk = hc_ref.shape[1]
    gw = w_ref.shape[1]
    t = n * blk + lax.broadcasted_iota(jnp.int32, (blk, 3 * blk), 0)
    j = (n - 1) * blk + lax.broadcasted_iota(jnp.int32, (blk, 3 * blk), 1)
    t_col = n * blk + lax.broadcasted_iota(jnp.int32, (blk, 1), 0)
    in_seq = (j >= 0) & (j < seq_len)
    for gi, win in enumerate(POOL_WINDOWS):
        gs = slice(gi * gw, (gi + 1) * gw)
        half = win // 2
        band = jnp.where(in_seq & (j >= t - half) & (j < t + half), 1.0, 0.0).astype(BF16)
        cnt = (jnp.minimum(t_col + half, seq_len) - jnp.maximum(t_col - half, 0)).astype(F32)
        x3 = jnp.concatenate([hp_ref[0, :, gs], hc_ref[0, :, gs], hn_ref[0, :, gs]], axis=0)
        mean = jnp.dot(band, x3, preferred_element_type=F32) / cnt
        dlt = (mean - hc_ref[0, :, gs].astype(F32)).astype(BF16)
        o_ref[0, :, gs] = (jnp.dot(dlt, w_ref[gi], preferred_element_type=F32) * sc_ref[:, gs]).astype(BF16)


def _pool_mixer(hp, w_pool, pool_scale):
    b, l, w = hp.shape
    blk = ATTN_BLOCK
    nb = l // blk
    cur = lambda i, j: (i, j, 0)
    prev = lambda i, j: (i, jnp.maximum(j - 1, 0), 0)
    nxt = lambda i, j: (i, jnp.minimum(j + 1, nb - 1), 0)
    return pl.pallas_call(
        functools.partial(_pool_kernel, seq_len=l),
        grid=(b, nb),
        in_specs=[
            pl.BlockSpec((1, blk, w), prev), pl.BlockSpec((1, blk, w), cur), pl.BlockSpec((1, blk, w), nxt),
            pl.BlockSpec(w_pool.shape, lambda i, j: (0, 0, 0)),
            pl.BlockSpec((1, w), lambda i, j: (0, 0)),
        ],
        out_specs=pl.BlockSpec((1, blk, w), cur),
        out_shape=jax.ShapeDtypeStruct((b, l, w), BF16),
        compiler_params=_cparams(("arbitrary", "arbitrary")),
        name="pool_mixer",
    )(hp, hp, hp, w_pool, pool_scale)


def _outproj_ln_kernel(a_ref, w_ref, x_ref, gate_ref, g_ref, b_ref, sh_ref, sc_ref, wr_ref,
                       x1_ref, h2_ref, lg_ref, *, alpha):
    y = jnp.dot(a_ref[0], w_ref[...], preferred_element_type=F32)
    x1 = _layer_norm_rows(alpha * x_ref[0] + gate_ref[0] * y, g_ref[...], b_ref[...])
    x1_ref[0] = x1
    h2 = x1 * (1.0 + sc_ref[0]) + sh_ref[0]
    h2_ref[0] = h2
    lg_ref[0] = jnp.dot(h2.astype(BF16), wr_ref[...], preferred_element_type=F32)


def _outproj_ln(a, w, x, gate, ln_g, ln_b, shift2, scale2, w_router_pad, *, alpha, tm):
    b, l, d = x.shape
    k = a.shape[2]
    row = lambda i, j: (i, j, 0)
    per_b = lambda i, j: (i, 0, 0)
    const2 = lambda i, j: (0, 0)
    return pl.pallas_call(
        functools.partial(_outproj_ln_kernel, alpha=alpha),
        grid=(b, l // tm),
        in_specs=[
            pl.BlockSpec((1, tm, k), row),
            pl.BlockSpec((k, d), const2),
            pl.BlockSpec((1, tm, d), row),
            pl.BlockSpec((1, 1, d), per_b),
            pl.BlockSpec((1, d), const2),
            pl.BlockSpec((1, d), const2),
            pl.BlockSpec((1, 1, d), per_b),
            pl.BlockSpec((1, 1, d), per_b),
            pl.BlockSpec((d, ROUTER_PAD), const2),
        ],
        out_specs=[pl.BlockSpec((1, tm, d), row), pl.BlockSpec((1, tm, d), row), pl.BlockSpec((1, tm, ROUTER_PAD), row)],
        out_shape=[
            jax.ShapeDtypeStruct((b, l, d), F32),
            jax.ShapeDtypeStruct((b, l, d), F32),
            jax.ShapeDtypeStruct((b, l, ROUTER_PAD), F32),
        ],
        compiler_params=_cparams(("arbitrary", "arbitrary")),
        name="outproj_ln",
    )(a, w, x, gate, ln_g, ln_b, shift2, scale2, w_router_pad)


def _residual_ln_kernel(x_ref, y_ref, gate_ref, g_ref, b_ref, o_ref, *, alpha):
    o_ref[0] = _layer_norm_rows(alpha * x_ref[0] + gate_ref[0] * y_ref[0], g_ref[...], b_ref[...])


def _residual_ln(x, y, gate, ln_g, ln_b, *, alpha, tm):
    b, l, d = x.shape
    row = lambda i, j: (i, j, 0)
    return pl.pallas_call(
        functools.partial(_residual_ln_kernel, alpha=alpha),
        grid=(b, l // tm),
        in_specs=[
            pl.BlockSpec((1, tm, d), row),
            pl.BlockSpec((1, tm, d), row),
            pl.BlockSpec((1, 1, d), lambda i, j: (i, 0, 0)),
            pl.BlockSpec((1, d), lambda i, j: (0, 0)),
            pl.BlockSpec((1, d), lambda i, j: (0, 0)),
        ],
        out_specs=pl.BlockSpec((1, tm, d), row),
        out_shape=jax.ShapeDtypeStruct((b, l, d), F32),
        compiler_params=_cparams(("arbitrary", "arbitrary")),
        name="residual_ln",
    )(x, y, gate, ln_g, ln_b)


def _moe_kernel(idx_ref, gates_ref, h_hbm, wg_ref, wu_ref, wd_ref, y_ref, xs_buf, sem, *, seq_len):
    b = pl.program_id(1)
    cap = xs_buf.shape[0]
    base = b * seq_len

    def row_copy(i):
        return pltpu.make_async_copy(h_hbm.at[pl.ds(base + idx_ref[0, 0, 0, i], 1)], xs_buf.at[pl.ds(i, 1)], sem)

    def issue(i, carry):
        row_copy(i).start()
        return carry

    lax.fori_loop(0, cap, issue, 0, unroll=8)
    pltpu.make_async_copy(h_hbm.at[pl.ds(0, cap)], xs_buf, sem).wait()

    xs = xs_buf[...].astype(BF16)
    a = jnp.dot(xs, wg_ref[0], preferred_element_type=F32)
    u = jnp.dot(xs, wu_ref[0], preferred_element_type=F32)
    act = (a * jax.nn.sigmoid(a) * u).astype(BF16)
    y_ref[0, 0] = jnp.dot(act, wd_ref[0], preferred_element_type=F32) * gates_ref[0, 0]


def _moe_experts(idx, gates, h2, w_gate, w_up, w_down):
    b, e, cap = idx.shape
    _, l, d = h2.shape
    f = w_gate.shape[2]
    return pl.pallas_call(
        functools.partial(_moe_kernel, seq_len=l),
        grid=(e, b),
        in_specs=[
            pl.BlockSpec((1, 1, 1, cap), lambda ei, bi: (bi, ei, 0, 0), memory_space=pltpu.SMEM),
            pl.BlockSpec((1, 1, cap, 1), lambda ei, bi: (bi, ei, 0, 0)),
            pl.BlockSpec(memory_space=pl.ANY),
            pl.BlockSpec((1, d, f), lambda ei, bi: (ei, 0, 0)),
            pl.BlockSpec((1, d, f), lambda ei, bi: (ei, 0, 0)),
            pl.BlockSpec((1, f, d), lambda ei, bi: (ei, 0, 0)),
        ],
        out_specs=pl.BlockSpec((1, 1, cap, d), lambda ei, bi: (bi, ei, 0, 0)),
        out_shape=jax.ShapeDtypeStruct((b, e, cap, d), F32),
        scratch_shapes=[pltpu.VMEM((cap, d), F32), pltpu.SemaphoreType.DMA(())],
        compiler_params=_cparams(("arbitrary", "arbitrary")),
        name="moe_experts",
    )(idx.reshape(b, e, 1, cap), gates.reshape(b, e, cap, 1), h2.reshape(b * l, d), w_gate, w_up, w_down)


def _expert_choice(h2, logits, w_gate, w_up, w_down):
    b, l, d = h2.shape
    cap = (CAPACITY_FACTOR * l) // N_EXPERTS
    aff = jax.nn.softmax(logits[..., :N_EXPERTS], axis=-1)
    gates, idx = lax.top_k(jnp.swapaxes(aff, 1, 2), cap)
    y = _moe_experts(idx.astype(jnp.int32), gates, h2, w_gate, w_up, w_down)
    return jax.vmap(lambda ib, yb: jnp.zeros((l, d), F32).at[ib.reshape(-1)].add(yb.reshape(-1, d)))(idx, y)


def _rope_tables(n_tokens):
    rows = n_tokens // GRID_W
    row = jnp.repeat(jnp.arange(rows, dtype=F32), GRID_W)
    col = jnp.tile(jnp.arange(GRID_W, dtype=F32), rows)
    inv = ROPE_BASE ** (-jnp.arange(0, AXIS_DIM, 2, dtype=F32) / AXIS_DIM)
    ang_r = row[:, None] * inv
    ang_c = col[:, None] * inv
    cos_t = jnp.concatenate([jnp.cos(ang_r), jnp.cos(ang_r), jnp.cos(ang_c), jnp.cos(ang_c)], axis=-1)
    sin_t = jnp.concatenate([-jnp.sin(ang_r), jnp.sin(ang_r), -jnp.sin(ang_c), jnp.sin(ang_c)], axis=-1)
    return cos_t, sin_t


def kernel(x, c, ctx, c_ctx, w_mod, b_mod, ln_g, ln_b, w_in_ab, attn_sink, w_spatial, b_spatial, gm_ln_g, gm_ln_b,
           w_out_ab, w_in_pool, w_pool, pool_scale, w_out_pool, w_router, w_gate, w_up, w_down):
    b, l, d = x.shape
    depth = w_mod.shape[0]
    assert depth == 2, "layer 0 is the attention/gMLP mixer, layer 1 the pooling mixer"
    alpha = (2.0 * depth) ** 0.25
    q_w = N_Q_HEADS * HEAD_DIM
    kv_w = N_KV_HEADS * HEAD_DIM
    gm_w = d // 2

    cs = jnp.zeros((MOD_ROWS, d), F32).at[:b].set(c).at[b].set(c_ctx)
    mod_all = _mod_vectors(cs, w_mod, b_mod)
    cos_t, sin_t = _rope_tables(l)
    w_router_pad = jnp.zeros((depth, d, ROUTER_PAD), BF16).at[:, :, :N_EXPERTS].set(w_router.astype(BF16))

    def mods(i):
        m = mod_all[i, :b].reshape(b, N_MOD, 1, d)
        return [m[:, t] for t in range(N_MOD)]

    mod = mods(0)
    modc = mod_all[0, b].reshape(N_MOD, 1, 1, d)
    w_in = w_in_ab[0].astype(BF16)
    q, k, v, gu, gv = _inproj_ab(x, mod[0], mod[1], w_in, cos_t, sin_t, tm=256)
    kvx = _modulated_matmul(ctx, modc[0], modc[1], w_in[:, q_w:q_w + 2 * kv_w], tm=ctx.shape[1], per_batch_mod=False)
    bs_full = jnp.broadcast_to(b_spatial[0].T[:, :, None], (CHUNK, GM_GROUPS, gm_w // GM_GROUPS)).reshape(CHUNK, gm_w)
    ab = _attn_gmlp(attn_sink[0], q, k, v, kvx, gu, gv, w_spatial[0].astype(BF16), bs_full,
                    gm_ln_g[0].reshape(1, gm_w), gm_ln_b[0].reshape(1, gm_w))
    x1, h2, logits = _outproj_ln(ab, w_out_ab[0].astype(BF16), x, mod[2], ln_g[0, 0].reshape(1, d),
                                 ln_b[0, 0].reshape(1, d), mod[3], mod[4], w_router_pad[0], alpha=alpha, tm=256)
    moe = _expert_choice(h2, logits, w_gate[0].astype(BF16), w_up[0].astype(BF16), w_down[0].astype(BF16))
    x2 = _residual_ln(x1, moe, mod[5], ln_g[0, 1].reshape(1, d), ln_b[0, 1].reshape(1, d), alpha=alpha, tm=512)

    mod = mods(1)
    hp = _modulated_matmul(x2, mod[0], mod[1], w_in_pool[0].astype(BF16), tm=256, per_batch_mod=True)
    pooled = _pool_mixer(hp, w_pool[0].astype(BF16), pool_scale[0].reshape(1, -1))
    x3, h2, logits = _outproj_ln(pooled, w_out_pool[0].astype(BF16), x2, mod[2], ln_g[1, 0].reshape(1, d),
                                 ln_b[1, 0].reshape(1, d), mod[3], mod[4], w_router_pad[1], alpha=alpha, tm=256)
    moe = _expert_choice(h2, logits, w_gate[1].astype(BF16), w_up[1].astype(BF16), w_down[1].astype(BF16))
    return _residual_ln(x3, moe, mod[5], ln_g[1, 1].reshape(1, d), ln_b[1, 1].reshape(1, d), alpha=alpha, tm=512)
```

```python
import functools

import jax
import jax.numpy as jnp
from jax import lax
from jax.experimental import pallas as pl
from jax.experimental.pallas import tpu as pltpu

F32 = jnp.float32
BF16 = jnp.bfloat16

GRID_W = 64
HEAD_DIM = 128
N_KV_HEADS = 2
GQA_GROUP = 4
N_Q_HEADS = N_KV_HEADS * GQA_GROUP
ATTN_BLOCK = 128
AXIS_DIM = HEAD_DIM // 2
ROPE_BASE = 10000.0
GM_GROUPS = 8
CHUNK = 128
POOL_WINDOWS = (2, 4, 8, 16)
N_EXPERTS = 16
CAPACITY_FACTOR = 2
N_MOD = 6
LN_EPS = 1e-6
NEG_INF = -1e30
INV_SQRT2 = 0.7071067811865476

V7X_LANES = 128
V7X_VMEM_LIMIT_BYTES = 56 * 1024 * 1024

MOD_ROWS = 32
ROUTER_PAD = V7X_LANES


def _cparams(sem):
    return pltpu.CompilerParams(dimension_semantics=sem, vmem_limit_bytes=V7X_VMEM_LIMIT_BYTES)


def _gelu(a):
    return 0.5 * a * (1.0 + lax.erf(a * INV_SQRT2))


def _layer_norm_rows(z, g, b):
    mu = jnp.mean(z, axis=-1, keepdims=True)
    zc = z - mu
    var = jnp.mean(zc * zc, axis=-1, keepdims=True)
    return zc * lax.rsqrt(var + LN_EPS) * g + b


def _mod_kernel(cs_ref, w_ref, b_ref, o_ref):
    cs = cs_ref[...]
    s = cs * jax.nn.sigmoid(cs)
    o_ref[0] = jnp.dot(s.astype(BF16), w_ref[0].astype(BF16), preferred_element_type=F32) + b_ref[0]


def _mod_vectors(cs, w_mod, b_mod):
    depth, d, n = w_mod.shape
    tn = 512
    return pl.pallas_call(
        _mod_kernel,
        grid=(depth, n // tn),
        in_specs=[
            pl.BlockSpec((MOD_ROWS, d), lambda i, j: (0, 0)),
            pl.BlockSpec((1, d, tn), lambda i, j: (i, 0, j)),
            pl.BlockSpec((1, 1, tn), lambda i, j: (i, 0, j)),
        ],
        out_specs=pl.BlockSpec((1, MOD_ROWS, tn), lambda i, j: (i, 0, j)),
        out_shape=jax.ShapeDtypeStruct((depth, MOD_ROWS, n), F32),
        compiler_params=_cparams(("arbitrary", "arbitrary")),
        name="mod_vectors",
    )(cs, w_mod, b_mod.reshape(depth, 1, n))


def _modmm_kernel(x_ref, sh_ref, sc_ref, w_ref, o_ref, *, n_chunk):
    h = (x_ref[0] * (1.0 + sc_ref[0]) + sh_ref[0]).astype(BF16)
    n = w_ref.shape[1]
    for c in range(n // n_chunk):
        sl = slice(c * n_chunk, (c + 1) * n_chunk)
        o_ref[0, :, sl] = jnp.dot(h, w_ref[:, sl], preferred_element_type=F32).astype(o_ref.dtype)


def _modulated_matmul(x, shift, scale, w, *, tm, per_batch_mod):
    b, l, d = x.shape
    n = w.shape[1]
    mod_map = (lambda i, j: (i, 0, 0)) if per_batch_mod else (lambda i, j: (0, 0, 0))
    return pl.pallas_call(
        functools.partial(_modmm_kernel, n_chunk=min(n, 512)),
        grid=(b, l // tm),
        in_specs=[
            pl.BlockSpec((1, tm, d), lambda i, j: (i, j, 0)),
            pl.BlockSpec((1, 1, d), mod_map),
            pl.BlockSpec((1, 1, d), mod_map),
            pl.BlockSpec((d, n), lambda i, j: (0, 0)),
        ],
        out_specs=pl.BlockSpec((1, tm, n), lambda i, j: (i, j, 0)),
        out_shape=jax.ShapeDtypeStruct((b, l, n), BF16),
        compiler_params=_cparams(("arbitrary", "arbitrary")),
        name="modulated_matmul",
    )(x, shift, scale, w)


def _inproj_ab_kernel(x_ref, sh_ref, sc_ref, w_ref, cos_ref, sin_ref,
                      q_ref, k_ref, v_ref, gu_ref, gv_ref, *, q_scale):
    h = (x_ref[0] * (1.0 + sc_ref[0]) + sh_ref[0]).astype(BF16)
    tm = h.shape[0]
    cos = cos_ref[...]
    sin = sin_ref[...]
    lane = lax.broadcasted_iota(jnp.int32, (tm, HEAD_DIM), 1)
    low_half = (lane & (AXIS_DIM // 2)) == 0

    def rope(a):
        partner = jnp.where(low_half, pltpu.roll(a, HEAD_DIM - AXIS_DIM // 2, 1), pltpu.roll(a, AXIS_DIM // 2, 1))
        return a * cos + partner * sin

    nc = 512
    q_w = q_ref.shape[2]
    kv_w = k_ref.shape[2]
    gm_w = gu_ref.shape[2]
    for c in range(w_ref.shape[1] // nc):
        acc = jnp.dot(h, w_ref[:, c * nc:(c + 1) * nc], preferred_element_type=F32)
        base = c * nc
        if base < q_w:
            for j in range(nc // HEAD_DIM):
                a = acc[:, j * HEAD_DIM:(j + 1) * HEAD_DIM]
                q_ref[0, :, base + j * HEAD_DIM: base + (j + 1) * HEAD_DIM] = (rope(a) * q_scale).astype(BF16)
        elif base < q_w + 2 * kv_w:
            for j in range(kv_w // HEAD_DIM):
                a = acc[:, j * HEAD_DIM:(j + 1) * HEAD_DIM]
                k_ref[0, :, j * HEAD_DIM:(j + 1) * HEAD_DIM] = rope(a).astype(BF16)
            v_ref[0] = acc[:, kv_w:2 * kv_w].astype(BF16)
        elif base < q_w + 2 * kv_w + gm_w:
            o = base - (q_w + 2 * kv_w)
            gu_ref[0, :, o:o + nc] = _gelu(acc).astype(BF16)
        else:
            o = base - (q_w + 2 * kv_w + gm_w)
            gv_ref[0, :, o:o + nc] = _gelu(acc).astype(BF16)


def _inproj_ab(x, shift, scale, w, cos_t, sin_t, *, tm):
    b, l, d = x.shape
    q_w = N_Q_HEADS * HEAD_DIM
    kv_w = N_KV_HEADS * HEAD_DIM
    gm_w = d // 2
    n = w.shape[1]
    assert n == q_w + 2 * kv_w + 2 * gm_w and 2 * kv_w == 512
    row = lambda i, j: (i, j, 0)
    return pl.pallas_call(
        functools.partial(_inproj_ab_kernel, q_scale=HEAD_DIM ** -0.5),
        grid=(b, l // tm),
        in_specs=[
            pl.BlockSpec((1, tm, d), row),
            pl.BlockSpec((1, 1, d), lambda i, j: (i, 0, 0)),
            pl.BlockSpec((1, 1, d), lambda i, j: (i, 0, 0)),
            pl.BlockSpec((d, n), lambda i, j: (0, 0)),
            pl.BlockSpec((tm, HEAD_DIM), lambda i, j: (j, 0)),
            pl.BlockSpec((tm, HEAD_DIM), lambda i, j: (j, 0)),
        ],
        out_specs=[
            pl.BlockSpec((1, tm, q_w), row),
            pl.BlockSpec((1, tm, kv_w), row),
            pl.BlockSpec((1, tm, kv_w), row),
            pl.BlockSpec((1, tm, gm_w), row),
            pl.BlockSpec((1, tm, gm_w), row),
        ],
        out_shape=[
            jax.ShapeDtypeStruct((b, l, q_w), BF16),
            jax.ShapeDtypeStruct((b, l, kv_w), BF16),
            jax.ShapeDtypeStruct((b, l, kv_w), BF16),
            jax.ShapeDtypeStruct((b, l, gm_w), BF16),
            jax.ShapeDtypeStruct((b, l, gm_w), BF16),
        ],
        compiler_params=_cparams(("arbitrary", "arbitrary")),
        name="inproj_ab",
    )(x, shift, scale, w, cos_t, sin_t)


def _attn_gmlp_kernel(sink_ref, q_ref, kp_ref, kc_ref, kn_ref, vp_ref, vc_ref, vn_ref, kvx_ref,
                      gu_ref, gv_ref, ws_ref, bs_ref, lng_ref, lnb_ref, o_ref, *, n_blocks):
    n = pl.program_id(1)
    blk = ATTN_BLOCK
    n_ctx = kvx_ref.shape[1]
    rows = GQA_GROUP * blk
    n_keys = 3 * blk + n_ctx
    kv_w = N_KV_HEADS * HEAD_DIM

    qi = lax.broadcasted_iota(jnp.int32, (rows, n_keys), 0) & (blk - 1)
    col = lax.broadcasted_iota(jnp.int32, (rows, n_keys), 1)
    off_prev = jnp.where(n > 0, 0, 1 << 20)
    off_next = jnp.where(n < n_blocks - 1, 0, 1 << 20)
    in_next = (col >= 2 * blk) & (col < 3 * blk)
    valid = ((col >= blk) | (col >= qi + off_prev)) & (jnp.logical_not(in_next) | (col - 2 * blk + off_next <= qi))
    head_of_row = lax.broadcasted_iota(jnp.int32, (rows, 1), 0) // blk

    for hk in range(N_KV_HEADS):
        hs = slice(hk * HEAD_DIM, (hk + 1) * HEAD_DIM)
        qs = jnp.concatenate(
            [q_ref[0, :, (hk * GQA_GROUP + g) * HEAD_DIM:(hk * GQA_GROUP + g + 1) * HEAD_DIM] for g in range(GQA_GROUP)],
            axis=0)
        k_all = jnp.concatenate([kp_ref[0, :, hs], kc_ref[0, :, hs], kn_ref[0, :, hs], kvx_ref[0, :, hs]], axis=0)
        v_all = jnp.concatenate([vp_ref[0, :, hs], vc_ref[0, :, hs], vn_ref[0, :, hs],
                                 kvx_ref[0, :, kv_w + hk * HEAD_DIM: kv_w + (hk + 1) * HEAD_DIM]], axis=0)
        s = lax.dot_general(qs, k_all, (((1,), (1,)), ((), ())), preferred_element_type=F32)
        s = jnp.where(valid, s, NEG_INF)
        sink = jnp.zeros((rows, 1), F32)
        for g in range(GQA_GROUP):
            sink = jnp.where(head_of_row == g, sink_ref[hk * GQA_GROUP + g], sink)
        m = jnp.maximum(jnp.max(s, axis=-1, keepdims=True), sink)
        p = jnp.exp(s - m)
        denom = jnp.sum(p, axis=-1, keepdims=True) + jnp.exp(sink - m)
        o = jnp.dot(p.astype(BF16), v_all, preferred_element_type=F32) / denom
        for g in range(GQA_GROUP):
            h = hk * GQA_GROUP + g
            o_ref[0, :, h * HEAD_DIM:(h + 1) * HEAD_DIM] = o[g * blk:(g + 1) * blk].astype(BF16)

    att_w = N_Q_HEADS * HEAD_DIM
    vn = _layer_norm_rows(gv_ref[0].astype(F32), lng_ref[...], lnb_ref[...]).astype(BF16)
    gw = vn.shape[1] // GM_GROUPS
    for g in range(GM_GROUPS):
        gs = slice(g * gw, (g + 1) * gw)
        mixed = jnp.dot(ws_ref[g], vn[:, gs], preferred_element_type=F32) + bs_ref[:, gs]
        o_ref[0, :, att_w + g * gw: att_w + (g + 1) * gw] = (gu_ref[0, :, gs].astype(F32) * mixed).astype(BF16)


def _attn_gmlp(sink, q, k, v, kvx, gu, gv, ws, bs_full, lng, lnb):
    b, l, q_w = q.shape
    kv_w = k.shape[2]
    gm_w = gu.shape[2]
    nb = l // ATTN_BLOCK
    n_ctx = kvx.shape[1]
    cur = lambda i, j: (i, j, 0)
    prev = lambda i, j: (i, jnp.maximum(j - 1, 0), 0)
    nxt = lambda i, j: (i, jnp.minimum(j + 1, nb - 1), 0)
    const2 = lambda i, j: (0, 0)
    kv_spec = lambda m: pl.BlockSpec((1, ATTN_BLOCK, kv_w), m)
    return pl.pallas_call(
        functools.partial(_attn_gmlp_kernel, n_blocks=nb),
        grid=(b, nb),
        in_specs=[
            pl.BlockSpec(memory_space=pltpu.SMEM),
            pl.BlockSpec((1, ATTN_BLOCK, q_w), cur),
            kv_spec(prev), kv_spec(cur), kv_spec(nxt),
            kv_spec(prev), kv_spec(cur), kv_spec(nxt),
            pl.BlockSpec((1, n_ctx, 2 * kv_w), lambda i, j: (i, 0, 0)),
            pl.BlockSpec((1, ATTN_BLOCK, gm_w), cur),
            pl.BlockSpec((1, ATTN_BLOCK, gm_w), cur),
            pl.BlockSpec((GM_GROUPS, CHUNK, CHUNK), lambda i, j: (0, 0, 0)),
            pl.BlockSpec((CHUNK, gm_w), const2),
            pl.BlockSpec((1, gm_w), const2),
            pl.BlockSpec((1, gm_w), const2),
        ],
        out_specs=pl.BlockSpec((1, ATTN_BLOCK, q_w + gm_w), cur),
        out_shape=jax.ShapeDtypeStruct((b, l, q_w + gm_w), BF16),
        compiler_params=_cparams(("arbitrary", "arbitrary")),
        name="attn_gmlp",
    )(sink, q, k, k, k, v, v, v, kvx, gu, gv, ws, bs_full, lng, lnb)


def _pool_kernel(hp_ref, hc_ref, hn_ref, w_ref, sc_ref, o_ref, *, seq_len):
    n = pl.program_id(1)
    blk = hc_ref.shape[1]
    gw = w_ref.shape[1]
    t = n * blk + lax.broadcasted_iota(jnp.int32, (blk, 3 * blk), 0)
    j = (n - 1) * blk + lax.broadcasted_iota(jnp.int32, (blk, 3 * blk), 1)
    t_col = n * blk + lax.broadcasted_iota(jnp.int32, (blk, 1), 0)
    in_seq = (j >= 0) & (j < seq_len)
    for gi, win in enumerate(POOL_WINDOWS):
        gs = slice(gi * gw, (gi + 1) * gw)
        half = win // 2
        band = jnp.where(in_seq & (j >= t - half) & (j < t + half), 1.0, 0.0).astype(BF16)
        cnt = (jnp.minimum(t_col + half, seq_len) - jnp.maximum(t_col - half, 0)).astype(F32)
        x3 = jnp.concatenate([hp_ref[0, :, gs], hc_ref[0, :, gs], hn_ref[0, :, gs]], axis=0)
        mean = jnp.dot(band, x3, preferred_element_type=F32) / cnt
        dlt = (mean - hc_ref[0, :, gs].astype(F32)).astype(BF16)
        o_ref[0, :, gs] = (jnp.dot(dlt, w_ref[gi], preferred_element_type=F32) * sc_ref[:, gs]).astype(BF16)


def _pool_mixer(hp, w_pool, pool_scale):
    b, l, w = hp.shape
    blk = ATTN_BLOCK
    nb = l // blk
    cur = lambda i, j: (i, j, 0)
    prev = lambda i, j: (i, jnp.maximum(j - 1, 0), 0)
    nxt = lambda i, j: (i, jnp.minimum(j + 1, nb - 1), 0)
    return pl.pallas_call(
        functools.partial(_pool_kernel, seq_len=l),
        grid=(b, nb),
        in_specs=[
            pl.BlockSpec((1, blk, w), prev), pl.BlockSpec((1, blk, w), cur), pl.BlockSpec((1, blk, w), nxt),
            pl.BlockSpec(w_pool.shape, lambda i, j: (0, 0, 0)),
            pl.BlockSpec((1, w), lambda i, j: (0, 0)),
        ],
        out_specs=pl.BlockSpec((1, blk, w), cur),
        out_shape=jax.ShapeDtypeStruct((b, l, w), BF16),
        compiler_params=_cparams(("arbitrary", "arbitrary")),
        name="pool_mixer",
    )(hp, hp, hp, w_pool, pool_scale)


def _outproj_ln_kernel(a_ref, w_ref, x_ref, gate_ref, g_ref, b_ref, sh_ref, sc_ref, wr_ref,
                       x1_ref, h2_ref, lg_ref, *, alpha):
    y = jnp.dot(a_ref[0], w_ref[...], preferred_element_type=F32)
    x1 = _layer_norm_rows(alpha * x_ref[0] + gate_ref[0] * y, g_ref[...], b_ref[...])
    x1_ref[0] = x1
    h2 = x1 * (1.0 + sc_ref[0]) + sh_ref[0]
    h2_ref[0] = h2
    lg_ref[0] = jnp.dot(h2.astype(BF16), wr_ref[...], preferred_element_type=F32)


def _outproj_ln(a, w, x, gate, ln_g, ln_b, shift2, scale2, w_router_pad, *, alpha, tm):
    b, l, d = x.shape
    k = a.shape[2]
    row = lambda i, j: (i, j, 0)
    per_b = lambda i, j: (i, 0, 0)
    const2 = lambda i, j: (0, 0)
    return pl.pallas_call(
        functools.partial(_outproj_ln_kernel, alpha=alpha),
        grid=(b, l // tm),
        in_specs=[
            pl.BlockSpec((1, tm, k), row),
            pl.BlockSpec((k, d), const2),
            pl.BlockSpec((1, tm, d), row),
            pl.BlockSpec((1, 1, d), per_b),
            pl.BlockSpec((1, d), const2),
            pl.BlockSpec((1, d), const2),
            pl.BlockSpec((1, 1, d), per_b),
            pl.BlockSpec((1, 1, d), per_b),
            pl.BlockSpec((d, ROUTER_PAD), const2),
        ],
        out_specs=[pl.BlockSpec((1, tm, d), row), pl.BlockSpec((1, tm, d), row), pl.BlockSpec((1, tm, ROUTER_PAD), row)],
        out_shape=[
            jax.ShapeDtypeStruct((b, l, d), F32),
            jax.ShapeDtypeStruct((b, l, d), F32),
            jax.ShapeDtypeStruct((b, l, ROUTER_PAD), F32),
        ],
        compiler_params=_cparams(("arbitrary", "arbitrary")),
        name="outproj_ln",
    )(a, w, x, gate, ln_g, ln_b, shift2, scale2, w_router_pad)


def _combine_ln_kernel(idx_ref, y_ref, x_ref, gate_ref, g_ref, b_ref, o_ref, *, alpha):
    tm = x_ref.shape[1]
    tok = pl.program_id(1) * tm + lax.broadcasted_iota(jnp.int32, (tm, 1), 0)
    onehot = jnp.where(idx_ref[0] == tok, 1.0, 0.0).astype(BF16)
    moe = jnp.dot(onehot, y_ref[0], preferred_element_type=F32)
    o_ref[0] = _layer_norm_rows(alpha * x_ref[0] + gate_ref[0] * moe, g_ref[...], b_ref[...])


def _combine_ln(idx, y, x, gate, ln_g, ln_b, *, alpha, tm):
    b, l, d = x.shape
    n_rows = y.shape[1]
    row = lambda i, j: (i, j, 0)
    per_b = lambda i, j: (i, 0, 0)
    return pl.pallas_call(
        functools.partial(_combine_ln_kernel, alpha=alpha),
        grid=(b, l // tm),
        in_specs=[
            pl.BlockSpec((1, 1, n_rows), per_b),
            pl.BlockSpec((1, n_rows, d), per_b),
            pl.BlockSpec((1, tm, d), row),
            pl.BlockSpec((1, 1, d), per_b),
            pl.BlockSpec((1, d), lambda i, j: (0, 0)),
            pl.BlockSpec((1, d), lambda i, j: (0, 0)),
        ],
        out_specs=pl.BlockSpec((1, tm, d), row),
        out_shape=jax.ShapeDtypeStruct((b, l, d), F32),
        compiler_params=_cparams(("arbitrary", "arbitrary")),
        name="combine_ln",
    )(idx.reshape(b, 1, n_rows), y, x, gate, ln_g, ln_b)


def _moe_kernel(idx_ref, idx_next_ref, gates_ref, h_hbm, wg_ref, wu_ref, wd_ref, y_ref, xs_buf, sem, *, seq_len):
    n_batch = pl.num_programs(1)
    n_steps = pl.num_programs(0) * n_batch
    step = pl.program_id(0) * n_batch + pl.program_id(1)
    slot = step % 2
    cap = xs_buf.shape[1]

    def start_gather(ids_ref, batch, dst_slot):
        base = batch * seq_len
        for i in range(cap):
            pltpu.make_async_copy(h_hbm.at[pl.ds(base + ids_ref[0, 0, 0, i], 1)],
                                  xs_buf.at[dst_slot, pl.ds(i, 1)], sem.at[dst_slot]).start()

    def wait_gather(dst_slot):
        pltpu.make_async_copy(h_hbm.at[pl.ds(0, cap)], xs_buf.at[dst_slot], sem.at[dst_slot]).wait()

    @pl.when(step == 0)
    def _():
        start_gather(idx_ref, pl.program_id(1), slot)

    wait_gather(slot)
    xs = xs_buf[slot].astype(BF16)
    next_step = jnp.minimum(step + 1, n_steps - 1)
    start_gather(idx_next_ref, next_step % n_batch, 1 - slot)

    a = jnp.dot(xs, wg_ref[0], preferred_element_type=F32)
    u = jnp.dot(xs, wu_ref[0], preferred_element_type=F32)
    act = (a * jax.nn.sigmoid(a) * u).astype(BF16)
    y_ref[0, 0] = (jnp.dot(act, wd_ref[0], preferred_element_type=F32) * gates_ref[0, 0]).astype(y_ref.dtype)

    @pl.when(step == n_steps - 1)
    def _():
        wait_gather(1 - slot)


def _moe_experts(idx, gates, h2, w_gate, w_up, w_down):
    b, e, cap = idx.shape
    _, l, d = h2.shape
    f = w_gate.shape[2]

    def next_idx_map(ei, bi):
        nxt = jnp.minimum(ei * b + bi + 1, e * b - 1)
        return (nxt % b, nxt // b, 0, 0)

    return pl.pallas_call(
        functools.partial(_moe_kernel, seq_len=l),
        grid=(e, b),
        in_specs=[
            pl.BlockSpec((1, 1, 1, cap), lambda ei, bi: (bi, ei, 0, 0), memory_space=pltpu.SMEM),
            pl.BlockSpec((1, 1, 1, cap), next_idx_map, memory_space=pltpu.SMEM),
            pl.BlockSpec((1, 1, cap, 1), lambda ei, bi: (bi, ei, 0, 0)),
            pl.BlockSpec(memory_space=pl.ANY),
            pl.BlockSpec((1, d, f), lambda ei, bi: (ei, 0, 0)),
            pl.BlockSpec((1, d, f), lambda ei, bi: (ei, 0, 0)),
            pl.BlockSpec((1, f, d), lambda ei, bi: (ei, 0, 0)),
        ],
        out_specs=pl.BlockSpec((1, 1, cap, d), lambda ei, bi: (bi, ei, 0, 0)),
        out_shape=jax.ShapeDtypeStruct((b, e, cap, d), BF16),
        scratch_shapes=[pltpu.VMEM((2, cap, d), F32), pltpu.SemaphoreType.DMA((2,))],
        compiler_params=_cparams(("arbitrary", "arbitrary")),
        name="moe_experts",
    )(idx.reshape(b, e, 1, cap), idx.reshape(b, e, 1, cap), gates.reshape(b, e, cap, 1), h2.reshape(b * l, d),
      w_gate, w_up, w_down)


def _expert_choice(h2, logits, w_gate, w_up, w_down):
    b, l, d = h2.shape
    cap = (CAPACITY_FACTOR * l) // N_EXPERTS
    aff = jax.nn.softmax(logits[..., :N_EXPERTS], axis=-1)
    gates, idx = lax.top_k(jnp.swapaxes(aff, 1, 2), cap)
    idx = idx.astype(jnp.int32)
    y = _moe_experts(idx, gates, h2, w_gate, w_up, w_down)
    return idx.reshape(b, N_EXPERTS * cap), y.reshape(b, N_EXPERTS * cap, d)


def _rope_tables(n_tokens):
    rows = n_tokens // GRID_W
    row = jnp.repeat(jnp.arange(rows, dtype=F32), GRID_W)
    col = jnp.tile(jnp.arange(GRID_W, dtype=F32), rows)
    inv = ROPE_BASE ** (-jnp.arange(0, AXIS_DIM, 2, dtype=F32) / AXIS_DIM)
    ang_r = row[:, None] * inv
    ang_c = col[:, None] * inv
    cos_t = jnp.concatenate([jnp.cos(ang_r), jnp.cos(ang_r), jnp.cos(ang_c), jnp.cos(ang_c)], axis=-1)
    sin_t = jnp.concatenate([-jnp.sin(ang_r), jnp.sin(ang_r), -jnp.sin(ang_c), jnp.sin(ang_c)], axis=-1)
    return cos_t, sin_t


def kernel(x, c, ctx, c_ctx, w_mod, b_mod, ln_g, ln_b, w_in_ab, attn_sink, w_spatial, b_spatial, gm_ln_g, gm_ln_b,
           w_out_ab, w_in_pool, w_pool, pool_scale, w_out_pool, w_router, w_gate, w_up, w_down):
    b, l, d = x.shape
    depth = w_mod.shape[0]
    assert depth == 2, "layer 0 is the attention/gMLP mixer, layer 1 the pooling mixer"
    alpha = (2.0 * depth) ** 0.25
    q_w = N_Q_HEADS * HEAD_DIM
    kv_w = N_KV_HEADS * HEAD_DIM
    gm_w = d // 2

    cs = jnp.zeros((MOD_ROWS, d), F32).at[:b].set(c).at[b].set(c_ctx)
    mod_all = _mod_vectors(cs, w_mod, b_mod)
    cos_t, sin_t = _rope_tables(l)
    w_router_pad = jnp.zeros((depth, d, ROUTER_PAD), BF16).at[:, :, :N_EXPERTS].set(w_router.astype(BF16))

    def mods(i):
        m = mod_all[i, :b].reshape(b, N_MOD, 1, d)
        return [m[:, t] for t in range(N_MOD)]

    mod = mods(0)
    modc = mod_all[0, b].reshape(N_MOD, 1, 1, d)
    w_in = w_in_ab[0].astype(BF16)
    q, k, v, gu, gv = _inproj_ab(x, mod[0], mod[1], w_in, cos_t, sin_t, tm=256)
    kvx = _modulated_matmul(ctx, modc[0], modc[1], w_in[:, q_w:q_w + 2 * kv_w], tm=ctx.shape[1], per_batch_mod=False)
    bs_full = jnp.broadcast_to(b_spatial[0].T[:, :, None], (CHUNK, GM_GROUPS, gm_w // GM_GROUPS)).reshape(CHUNK, gm_w)
    ab = _attn_gmlp(attn_sink[0], q, k, v, kvx, gu, gv, w_spatial[0].astype(BF16), bs_full,
                    gm_ln_g[0].reshape(1, gm_w), gm_ln_b[0].reshape(1, gm_w))
    x1, h2, logits = _outproj_ln(ab, w_out_ab[0].astype(BF16), x, mod[2], ln_g[0, 0].reshape(1, d),
                                 ln_b[0, 0].reshape(1, d), mod[3], mod[4], w_router_pad[0], alpha=alpha, tm=256)
    idx, y = _expert_choice(h2, logits, w_gate[0].astype(BF16), w_up[0].astype(BF16), w_down[0].astype(BF16))
    x2 = _combine_ln(idx, y, x1, mod[5], ln_g[0, 1].reshape(1, d), ln_b[0, 1].reshape(1, d), alpha=alpha, tm=256)

    mod = mods(1)
    hp = _modulated_matmul(x2, mod[0], mod[1], w_in_pool[0].astype(BF16), tm=256, per_batch_mod=True)
    pooled = _pool_mixer(hp, w_pool[0].astype(BF16), pool_scale[0].reshape(1, -1))
    x3, h2, logits = _outproj_ln(pooled, w_out_pool[0].astype(BF16), x2, mod[2], ln_g[1, 0].reshape(1, d),
                                 ln_b[1, 0].reshape(1, d), mod[3], mod[4], w_router_pad[1], alpha=alpha, tm=256)
    idx, y = _expert_choice(h2, logits, w_gate[1].astype(BF16), w_up[1].astype(BF16), w_down[1].astype(BF16))
    return _combine_ln(idx, y, x3, mod[5], ln_g[1, 1].reshape(1, d), ln_b[1, 1].reshape(1, d), alpha=alpha, tm=256)
```

```python
import functools

import jax
import jax.numpy as jnp
from jax import lax
from jax.experimental import pallas as pl
from jax.experimental.pallas import tpu as pltpu

F32 = jnp.float32
BF16 = jnp.bfloat16

GRID_W = 64
HEAD_DIM = 128
N_KV_HEADS = 2
GQA_GROUP = 4
N_Q_HEADS = N_KV_HEADS * GQA_GROUP
ATTN_BLOCK = 128
AXIS_DIM = HEAD_DIM // 2
ROPE_BASE = 10000.0
GM_GROUPS = 8
CHUNK = 128
POOL_WINDOWS = (2, 4, 8, 16)
N_EXPERTS = 16
CAPACITY_FACTOR = 2
N_MOD = 6
LN_EPS = 1e-6
NEG_INF = -1e30
INV_SQRT2 = 0.7071067811865476

V7X_LANES = 128
V7X_VMEM_LIMIT_BYTES = 56 * 1024 * 1024

MOD_ROWS = 32
ROUTER_PAD = V7X_LANES


def _cparams(sem):
    return pltpu.CompilerParams(dimension_semantics=sem, vmem_limit_bytes=V7X_VMEM_LIMIT_BYTES)


def _gelu(a):
    return 0.5 * a * (1.0 + lax.erf(a * INV_SQRT2))


def _layer_norm_rows(z, g, b):
    mu = jnp.mean(z, axis=-1, keepdims=True)
    zc = z - mu
    var = jnp.mean(zc * zc, axis=-1, keepdims=True)
    return zc * lax.rsqrt(var + LN_EPS) * g + b


def _mod_kernel(cs_ref, w_ref, b_ref, o_ref):
    cs = cs_ref[...]
    s = cs * jax.nn.sigmoid(cs)
    o_ref[0] = jnp.dot(s.astype(BF16), w_ref[0].astype(BF16), preferred_element_type=F32) + b_ref[0]


def _mod_vectors(cs, w_mod, b_mod):
    depth, d, n = w_mod.shape
    tn = 512
    return pl.pallas_call(
        _mod_kernel,
        grid=(depth, n // tn),
        in_specs=[
            pl.BlockSpec((MOD_ROWS, d), lambda i, j: (0, 0)),
            pl.BlockSpec((1, d, tn), lambda i, j: (i, 0, j)),
            pl.BlockSpec((1, 1, tn), lambda i, j: (i, 0, j)),
        ],
        out_specs=pl.BlockSpec((1, MOD_ROWS, tn), lambda i, j: (i, 0, j)),
        out_shape=jax.ShapeDtypeStruct((depth, MOD_ROWS, n), F32),
        compiler_params=_cparams(("arbitrary", "arbitrary")),
        name="mod_vectors",
    )(cs, w_mod, b_mod.reshape(depth, 1, n))


def _modmm_kernel(x_ref, sh_ref, sc_ref, w_ref, o_ref, *, n_chunk):
    h = (x_ref[0] * (1.0 + sc_ref[0]) + sh_ref[0]).astype(BF16)
    n = w_ref.shape[1]
    for c in range(n // n_chunk):
        sl = slice(c * n_chunk, (c + 1) * n_chunk)
        o_ref[0, :, sl] = jnp.dot(h, w_ref[:, sl], preferred_element_type=F32).astype(o_ref.dtype)


def _modulated_matmul(x, shift, scale, w, *, tm, per_batch_mod):
    b, l, d = x.shape
    n = w.shape[1]
    mod_map = (lambda i, j: (i, 0, 0)) if per_batch_mod else (lambda i, j: (0, 0, 0))
    return pl.pallas_call(
        functools.partial(_modmm_kernel, n_chunk=min(n, 512)),
        grid=(b, l // tm),
        in_specs=[
            pl.BlockSpec((1, tm, d), lambda i, j: (i, j, 0)),
            pl.BlockSpec((1, 1, d), mod_map),
            pl.BlockSpec((1, 1, d), mod_map),
            pl.BlockSpec((d, n), lambda i, j: (0, 0)),
        ],
        out_specs=pl.BlockSpec((1, tm, n), lambda i, j: (i, j, 0)),
        out_shape=jax.ShapeDtypeStruct((b, l, n), BF16),
        compiler_params=_cparams(("arbitrary", "arbitrary")),
        name="modulated_matmul",
    )(x, shift, scale, w)


def _inproj_ab_kernel(x_ref, sh_ref, sc_ref, w_ref, cos_ref, sin_ref,
                      q_ref, k_ref, v_ref, gu_ref, gv_ref, *, q_scale):
    h = (x_ref[0] * (1.0 + sc_ref[0]) + sh_ref[0]).astype(BF16)
    tm = h.shape[0]
    cos = cos_ref[...]
    sin = sin_ref[...]
    lane = lax.broadcasted_iota(jnp.int32, (tm, HEAD_DIM), 1)
    low_half = (lane & (AXIS_DIM // 2)) == 0

    def rope(a):
        partner = jnp.where(low_half, pltpu.roll(a, HEAD_DIM - AXIS_DIM // 2, 1), pltpu.roll(a, AXIS_DIM // 2, 1))
        return a * cos + partner * sin

    nc = 512
    q_w = q_ref.shape[2]
    kv_w = k_ref.shape[2]
    gm_w = gu_ref.shape[2]
    for c in range(w_ref.shape[1] // nc):
        acc = jnp.dot(h, w_ref[:, c * nc:(c + 1) * nc], preferred_element_type=F32)
        base = c * nc
        if base < q_w:
            for j in range(nc // HEAD_DIM):
                a = acc[:, j * HEAD_DIM:(j + 1) * HEAD_DIM]
                q_ref[0, :, base + j * HEAD_DIM: base + (j + 1) * HEAD_DIM] = (rope(a) * q_scale).astype(BF16)
        elif base < q_w + 2 * kv_w:
            for j in range(kv_w // HEAD_DIM):
                a = acc[:, j * HEAD_DIM:(j + 1) * HEAD_DIM]
                k_ref[0, :, j * HEAD_DIM:(j + 1) * HEAD_DIM] = rope(a).astype(BF16)
            v_ref[0] = acc[:, kv_w:2 * kv_w].astype(BF16)
        elif base < q_w + 2 * kv_w + gm_w:
            o = base - (q_w + 2 * kv_w)
            gu_ref[0, :, o:o + nc] = _gelu(acc).astype(BF16)
        else:
            o = base - (q_w + 2 * kv_w + gm_w)
            gv_ref[0, :, o:o + nc] = _gelu(acc).astype(BF16)


def _inproj_ab(x, shift, scale, w, cos_t, sin_t, *, tm):
    b, l, d = x.shape
    q_w = N_Q_HEADS * HEAD_DIM
    kv_w = N_KV_HEADS * HEAD_DIM
    gm_w = d // 2
    n = w.shape[1]
    assert n == q_w + 2 * kv_w + 2 * gm_w and 2 * kv_w == 512
    row = lambda i, j: (i, j, 0)
    return pl.pallas_call(
        functools.partial(_inproj_ab_kernel, q_scale=HEAD_DIM ** -0.5),
        grid=(b, l // tm),
        in_specs=[
            pl.BlockSpec((1, tm, d), row),
            pl.BlockSpec((1, 1, d), lambda i, j: (i, 0, 0)),
            pl.BlockSpec((1, 1, d), lambda i, j: (i, 0, 0)),
            pl.BlockSpec((d, n), lambda i, j: (0, 0)),
            pl.BlockSpec((tm, HEAD_DIM), lambda i, j: (j, 0)),
            pl.BlockSpec((tm, HEAD_DIM), lambda i, j: (j, 0)),
        ],
        out_specs=[
            pl.BlockSpec((1, tm, q_w), row),
            pl.BlockSpec((1, tm, kv_w), row),
            pl.BlockSpec((1, tm, kv_w), row),
            pl.BlockSpec((1, tm, gm_w), row),
            pl.BlockSpec((1, tm, gm_w), row),
        ],
        out_shape=[
            jax.ShapeDtypeStruct((b, l, q_w), BF16),
            jax.ShapeDtypeStruct((b, l, kv_w), BF16),
            jax.ShapeDtypeStruct((b, l, kv_w), BF16),
            jax.ShapeDtypeStruct((b, l, gm_w), BF16),
            jax.ShapeDtypeStruct((b, l, gm_w), BF16),
        ],
        compiler_params=_cparams(("arbitrary", "arbitrary")),
        name="inproj_ab",
    )(x, shift, scale, w, cos_t, sin_t)


def _attn_gmlp_kernel(sink_ref, q_ref, kp_ref, kc_ref, kn_ref, vp_ref, vc_ref, vn_ref, kvx_ref,
                      gu_ref, gv_ref, ws_ref, bs_ref, lng_ref, lnb_ref, o_ref, *, n_blocks):
    n = pl.program_id(1)
    blk = ATTN_BLOCK
    n_ctx = kvx_ref.shape[1]
    rows = GQA_GROUP * blk
    n_keys = 3 * blk + n_ctx
    kv_w = N_KV_HEADS * HEAD_DIM

    qi = lax.broadcasted_iota(jnp.int32, (rows, n_keys), 0) & (blk - 1)
    col = lax.broadcasted_iota(jnp.int32, (rows, n_keys), 1)
    off_prev = jnp.where(n > 0, 0, 1 << 20)
    off_next = jnp.where(n < n_blocks - 1, 0, 1 << 20)
    in_next = (col >= 2 * blk) & (col < 3 * blk)
    valid = ((col >= blk) | (col >= qi + off_prev)) & (jnp.logical_not(in_next) | (col - 2 * blk + off_next <= qi))
    head_of_row = lax.broadcasted_iota(jnp.int32, (rows, 1), 0) // blk

    for hk in range(N_KV_HEADS):
        hs = slice(hk * HEAD_DIM, (hk + 1) * HEAD_DIM)
        qs = jnp.concatenate(
            [q_ref[0, :, (hk * GQA_GROUP + g) * HEAD_DIM:(hk * GQA_GROUP + g + 1) * HEAD_DIM] for g in range(GQA_GROUP)],
            axis=0)
        k_all = jnp.concatenate([kp_ref[0, :, hs], kc_ref[0, :, hs], kn_ref[0, :, hs], kvx_ref[0, :, hs]], axis=0)
        v_all = jnp.concatenate([vp_ref[0, :, hs], vc_ref[0, :, hs], vn_ref[0, :, hs],
                                 kvx_ref[0, :, kv_w + hk * HEAD_DIM: kv_w + (hk + 1) * HEAD_DIM]], axis=0)
        s = lax.dot_general(qs, k_all, (((1,), (1,)), ((), ())), preferred_element_type=F32)
        s = jnp.where(valid, s, NEG_INF)
        sink = jnp.zeros((rows, 1), F32)
        for g in range(GQA_GROUP):
            sink = jnp.where(head_of_row == g, sink_ref[hk * GQA_GROUP + g], sink)
        m = jnp.maximum(jnp.max(s, axis=-1, keepdims=True), sink)
        p = jnp.exp(s - m)
        denom = jnp.sum(p, axis=-1, keepdims=True) + jnp.exp(sink - m)
        o = jnp.dot(p.astype(BF16), v_all, preferred_element_type=F32) / denom
        for g in range(GQA_GROUP):
            h = hk * GQA_GROUP + g
            o_ref[0, :, h * HEAD_DIM:(h + 1) * HEAD_DIM] = o[g * blk:(g + 1) * blk].astype(BF16)

    att_w = N_Q_HEADS * HEAD_DIM
    vn = _layer_norm_rows(gv_ref[0].astype(F32), lng_ref[...], lnb_ref[...]).astype(BF16)
    gw = vn.shape[1] // GM_GROUPS
    for g in range(GM_GROUPS):
        gs = slice(g * gw, (g + 1) * gw)
        mixed = jnp.dot(ws_ref[g], vn[:, gs], preferred_element_type=F32) + bs_ref[:, gs]
        o_ref[0, :, att_w + g * gw: att_w + (g + 1) * gw] = (gu_ref[0, :, gs].astype(F32) * mixed).astype(BF16)


def _attn_gmlp(sink, q, k, v, kvx, gu, gv, ws, bs_full, lng, lnb):
    b, l, q_w = q.shape
    kv_w = k.shape[2]
    gm_w = gu.shape[2]
    nb = l // ATTN_BLOCK
    n_ctx = kvx.shape[1]
    cur = lambda i, j: (i, j, 0)
    prev = lambda i, j: (i, jnp.maximum(j - 1, 0), 0)
    nxt = lambda i, j: (i, jnp.minimum(j + 1, nb - 1), 0)
    const2 = lambda i, j: (0, 0)
    kv_spec = lambda m: pl.BlockSpec((1, ATTN_BLOCK, kv_w), m)
    return pl.pallas_call(
        functools.partial(_attn_gmlp_kernel, n_blocks=nb),
        grid=(b, nb),
        in_specs=[
            pl.BlockSpec(memory_space=pltpu.SMEM),
            pl.BlockSpec((1, ATTN_BLOCK, q_w), cur),
            kv_spec(prev), kv_spec(cur), kv_spec(nxt),
            kv_spec(prev), kv_spec(cur), kv_spec(nxt),
            pl.BlockSpec((1, n_ctx, 2 * kv_w), lambda i, j: (i, 0, 0)),
            pl.BlockSpec((1, ATTN_BLOCK, gm_w), cur),
            pl.BlockSpec((1, ATTN_BLOCK, gm_w), cur),
            pl.BlockSpec((GM_GROUPS, CHUNK, CHUNK), lambda i, j: (0, 0, 0)),
            pl.BlockSpec((CHUNK, gm_w), const2),
            pl.BlockSpec((1, gm_w), const2),
            pl.BlockSpec((1, gm_w), const2),
        ],
        out_specs=pl.BlockSpec((1, ATTN_BLOCK, q_w + gm_w), cur),
        out_shape=jax.ShapeDtypeStruct((b, l, q_w + gm_w), BF16),
        compiler_params=_cparams(("arbitrary", "arbitrary")),
        name="attn_gmlp",
    )(sink, q, k, k, k, v, v, v, kvx, gu, gv, ws, bs_full, lng, lnb)


def _pool_kernel(hp_ref, hc_ref, hn_ref, w_ref, sc_ref, o_ref, *, seq_len):
    n = pl.program_id(1)
    blk = hc_ref.shape[1]
    gw = w_ref.shape[1]
    t = n * blk + lax.broadcasted_iota(jnp.int32, (blk, 3 * blk), 0)
    j = (n - 1) * blk + lax.broadcasted_iota(jnp.int32, (blk, 3 * blk), 1)
    t_col = n * blk + lax.broadcasted_iota(jnp.int32, (blk, 1), 0)
    in_seq = (j >= 0) & (j < seq_len)
    for gi, win in enumerate(POOL_WINDOWS):
        gs = slice(gi * gw, (gi + 1) * gw)
        half = win // 2
        band = jnp.where(in_seq & (j >= t - half) & (j < t + half), 1.0, 0.0).astype(BF16)
        cnt = (jnp.minimum(t_col + half, seq_len) - jnp.maximum(t_col - half, 0)).astype(F32)
        x3 = jnp.concatenate([hp_ref[0, :, gs], hc_ref[0, :, gs], hn_ref[0, :, gs]], axis=0)
        mean = jnp.dot(band, x3, preferred_element_type=F32) / cnt
        dlt = (mean - hc_ref[0, :, gs].astype(F32)).astype(BF16)
        o_ref[0, :, gs] = (jnp.dot(dlt, w_ref[gi], preferred_element_type=F32) * sc_ref[:, gs]).astype(BF16)


def _pool_mixer(hp, w_pool, pool_scale):
    b, l, w = hp.shape
    blk = ATTN_BLOCK
    nb = l // blk
    cur = lambda i, j: (i, j, 0)
    prev = lambda i, j: (i, jnp.maximum(j - 1, 0), 0)
    nxt = lambda i, j: (i, jnp.minimum(j + 1, nb - 1), 0)
    return pl.pallas_call(
        functools.partial(_pool_kernel, seq_len=l),
        grid=(b, nb),
        in_specs=[
            pl.BlockSpec((1, blk, w), prev), pl.BlockSpec((1, blk, w), cur), pl.BlockSpec((1, blk, w), nxt),
            pl.BlockSpec(w_pool.shape, lambda i, j: (0, 0, 0)),
            pl.BlockSpec((1, w), lambda i, j: (0, 0)),
        ],
        out_specs=pl.BlockSpec((1, blk, w), cur),
        out_shape=jax.ShapeDtypeStruct((b, l, w), BF16),
        compiler_params=_cparams(("arbitrary", "arbitrary")),
        name="pool_mixer",
    )(hp, hp, hp, w_pool, pool_scale)


def _outproj_ln_kernel(a_ref, w_ref, x_ref, gate_ref, g_ref, b_ref, sh_ref, sc_ref, wr_ref,
                       x1_ref, h2_ref, lg_ref, *, alpha):
    y = jnp.dot(a_ref[0], w_ref[...], preferred_element_type=F32)
    x1 = _layer_norm_rows(alpha * x_ref[0] + gate_ref[0] * y, g_ref[...], b_ref[...])
    x1_ref[0] = x1
    h2 = x1 * (1.0 + sc_ref[0]) + sh_ref[0]
    h2_ref[0] = h2
    lg_ref[0] = jnp.dot(h2.astype(BF16), wr_ref[...], preferred_element_type=F32)


def _outproj_ln(a, w, x, gate, ln_g, ln_b, shift2, scale2, w_router_pad, *, alpha, tm):
    b, l, d = x.shape
    k = a.shape[2]
    row = lambda i, j: (i, j, 0)
    per_b = lambda i, j: (i, 0, 0)
    const2 = lambda i, j: (0, 0)
    return pl.pallas_call(
        functools.partial(_outproj_ln_kernel, alpha=alpha),
        grid=(b, l // tm),
        in_specs=[
            pl.BlockSpec((1, tm, k), row),
            pl.BlockSpec((k, d), const2),
            pl.BlockSpec((1, tm, d), row),
            pl.BlockSpec((1, 1, d), per_b),
            pl.BlockSpec((1, d), const2),
            pl.BlockSpec((1, d), const2),
            pl.BlockSpec((1, 1, d), per_b),
            pl.BlockSpec((1, 1, d), per_b),
            pl.BlockSpec((d, ROUTER_PAD), const2),
        ],
        out_specs=[pl.BlockSpec((1, tm, d), row), pl.BlockSpec((1, tm, d), row), pl.BlockSpec((1, tm, ROUTER_PAD), row)],
        out_shape=[
            jax.ShapeDtypeStruct((b, l, d), F32),
            jax.ShapeDtypeStruct((b, l, d), F32),
            jax.ShapeDtypeStruct((b, l, ROUTER_PAD), F32),
        ],
        compiler_params=_cparams(("arbitrary", "arbitrary")),
        name="outproj_ln",
    )(a, w, x, gate, ln_g, ln_b, shift2, scale2, w_router_pad)


def _combine_ln_kernel(idx_ref, y_ref, x_ref, gate_ref, g_ref, b_ref, o_ref, *, alpha):
    tm = x_ref.shape[1]
    tok = pl.program_id(1) * tm + lax.broadcasted_iota(jnp.int32, (tm, 1), 0)
    onehot = jnp.where(idx_ref[0] == tok, 1.0, 0.0).astype(BF16)
    moe = jnp.dot(onehot, y_ref[0], preferred_element_type=F32)
    o_ref[0] = _layer_norm_rows(alpha * x_ref[0] + gate_ref[0] * moe, g_ref[...], b_ref[...])


def _combine_ln(idx, y, x, gate, ln_g, ln_b, *, alpha, tm):
    b, l, d = x.shape
    n_rows = y.shape[1]
    row = lambda i, j: (i, j, 0)
    per_b = lambda i, j: (i, 0, 0)
    return pl.pallas_call(
        functools.partial(_combine_ln_kernel, alpha=alpha),
        grid=(b, l // tm),
        in_specs=[
            pl.BlockSpec((1, 1, n_rows), per_b),
            pl.BlockSpec((1, n_rows, d), per_b),
            pl.BlockSpec((1, tm, d), row),
            pl.BlockSpec((1, 1, d), per_b),
            pl.BlockSpec((1, d), lambda i, j: (0, 0)),
            pl.BlockSpec((1, d), lambda i, j: (0, 0)),
        ],
        out_specs=pl.BlockSpec((1, tm, d), row),
        out_shape=jax.ShapeDtypeStruct((b, l, d), F32),
        compiler_params=_cparams(("arbitrary", "arbitrary")),
        name="combine_ln",
    )(idx.reshape(b, 1, n_rows), y, x, gate, ln_g, ln_b)


def _moe_kernel(idx_ref, idx_next_ref, gates_ref, h_hbm, wg_hbm, wu_hbm, wd_hbm, y_ref,
                xs_buf, wg_buf, wu_buf, wd_buf, wg_stage, wu_stage, wd_stage, sem, wsem, *, seq_len, layer):
    e = pl.program_id(0)
    bi = pl.program_id(1)
    n_exp = pl.num_programs(0)
    n_batch = pl.num_programs(1)
    n_steps = n_exp * n_batch
    step = e * n_batch + bi
    slot = step % 2
    cap = xs_buf.shape[1]
    d_rows = wg_stage.shape[0]
    f_rows = wd_stage.shape[0]

    def weight_copies(expert, chunk):
        dr = pl.ds(pl.multiple_of(chunk * d_rows, d_rows), d_rows)
        fr = pl.ds(pl.multiple_of(chunk * f_rows, f_rows), f_rows)
        return (pltpu.make_async_copy(wg_hbm.at[layer, expert, dr], wg_stage, wsem.at[0]),
                pltpu.make_async_copy(wu_hbm.at[layer, expert, dr], wu_stage, wsem.at[1]),
                pltpu.make_async_copy(wd_hbm.at[layer, expert, fr], wd_stage, wsem.at[2]))

    def convert_chunk(w_slot, chunk):
        dr = pl.ds(pl.multiple_of(chunk * d_rows, d_rows), d_rows)
        fr = pl.ds(pl.multiple_of(chunk * f_rows, f_rows), f_rows)
        wg_buf[w_slot, dr] = wg_stage[...].astype(BF16)
        wu_buf[w_slot, dr] = wu_stage[...].astype(BF16)
        wd_buf[w_slot, fr] = wd_stage[...].astype(BF16)

    def start_gather(ids_ref, batch, dst_slot):
        base = batch * seq_len
        for i in range(cap):
            pltpu.make_async_copy(h_hbm.at[pl.ds(base + ids_ref[0, 0, 0, i], 1)],
                                  xs_buf.at[dst_slot, pl.ds(i, 1)], sem.at[dst_slot]).start()

    def wait_gather(dst_slot):
        pltpu.make_async_copy(h_hbm.at[pl.ds(0, cap)], xs_buf.at[dst_slot], sem.at[dst_slot]).wait()

    @pl.when(step == 0)
    def _():
        start_gather(idx_ref, bi, slot)

        def load_chunk(chunk, carry):
            copies = weight_copies(0, chunk)
            for cp in copies:
                cp.start()
            for cp in copies:
                cp.wait()
            convert_chunk(0, chunk)
            return carry

        lax.fori_loop(0, n_batch, load_chunk, 0)

    next_expert = jnp.minimum(e + 1, n_exp - 1)
    for cp in weight_copies(next_expert, bi):
        cp.start()
    next_step = jnp.minimum(step + 1, n_steps - 1)
    start_gather(idx_next_ref, next_step % n_batch, 1 - slot)
    wait_gather(slot)

    w_slot = e % 2
    xs = xs_buf[slot].astype(BF16)
    a = jnp.dot(xs, wg_buf[w_slot], preferred_element_type=F32)
    u = jnp.dot(xs, wu_buf[w_slot], preferred_element_type=F32)
    act = (a * jax.nn.sigmoid(a) * u).astype(BF16)
    y_ref[0, 0] = (jnp.dot(act, wd_buf[w_slot], preferred_element_type=F32) * gates_ref[0, 0]).astype(y_ref.dtype)

    for cp in weight_copies(next_expert, bi):
        cp.wait()
    convert_chunk(1 - w_slot, bi)

    @pl.when(step == n_steps - 1)
    def _():
        wait_gather(1 - slot)


def _moe_experts(idx, gates, h2, w_gate, w_up, w_down, *, layer):
    b, e, cap = idx.shape
    _, l, d = h2.shape
    f = w_gate.shape[3]
    assert d % b == 0 and f % b == 0, "one weight row chunk per sample step"

    def next_idx_map(ei, bi):
        nxt = jnp.minimum(ei * b + bi + 1, e * b - 1)
        return (nxt % b, nxt // b, 0, 0)

    return pl.pallas_call(
        functools.partial(_moe_kernel, seq_len=l, layer=layer),
        grid=(e, b),
        in_specs=[
            pl.BlockSpec((1, 1, 1, cap), lambda ei, bi: (bi, ei, 0, 0), memory_space=pltpu.SMEM),
            pl.BlockSpec((1, 1, 1, cap), next_idx_map, memory_space=pltpu.SMEM),
            pl.BlockSpec((1, 1, cap, 1), lambda ei, bi: (bi, ei, 0, 0)),
            pl.BlockSpec(memory_space=pl.ANY),
            pl.BlockSpec(memory_space=pl.ANY),
            pl.BlockSpec(memory_space=pl.ANY),
            pl.BlockSpec(memory_space=pl.ANY),
        ],
        out_specs=pl.BlockSpec((1, 1, cap, d), lambda ei, bi: (bi, ei, 0, 0)),
        out_shape=jax.ShapeDtypeStruct((b, e, cap, d), BF16),
        scratch_shapes=[
            pltpu.VMEM((2, cap, d), F32),
            pltpu.VMEM((2, d, f), BF16), pltpu.VMEM((2, d, f), BF16), pltpu.VMEM((2, f, d), BF16),
            pltpu.VMEM((d // b, f), F32), pltpu.VMEM((d // b, f), F32), pltpu.VMEM((f // b, d), F32),
            pltpu.SemaphoreType.DMA((2,)), pltpu.SemaphoreType.DMA((3,)),
        ],
        compiler_params=_cparams(("arbitrary", "arbitrary")),
        name="moe_experts",
    )(idx.reshape(b, e, 1, cap), idx.reshape(b, e, 1, cap), gates.reshape(b, e, cap, 1), h2.reshape(b * l, d),
      w_gate, w_up, w_down)


def _expert_choice(h2, logits, w_gate, w_up, w_down, *, layer):
    b, l, d = h2.shape
    cap = (CAPACITY_FACTOR * l) // N_EXPERTS
    aff = jax.nn.softmax(logits[..., :N_EXPERTS], axis=-1)
    gates, idx = lax.top_k(jnp.swapaxes(aff, 1, 2), cap)
    idx = idx.astype(jnp.int32)
    y = _moe_experts(idx, gates, h2, w_gate, w_up, w_down, layer=layer)
    return idx.reshape(b, N_EXPERTS * cap), y.reshape(b, N_EXPERTS * cap, d)


def _rope_tables(n_tokens):
    rows = n_tokens // GRID_W
    row = jnp.repeat(jnp.arange(rows, dtype=F32), GRID_W)
    col = jnp.tile(jnp.arange(GRID_W, dtype=F32), rows)
    inv = ROPE_BASE ** (-jnp.arange(0, AXIS_DIM, 2, dtype=F32) / AXIS_DIM)
    ang_r = row[:, None] * inv
    ang_c = col[:, None] * inv
    cos_t = jnp.concatenate([jnp.cos(ang_r), jnp.cos(ang_r), jnp.cos(ang_c), jnp.cos(ang_c)], axis=-1)
    sin_t = jnp.concatenate([-jnp.sin(ang_r), jnp.sin(ang_r), -jnp.sin(ang_c), jnp.sin(ang_c)], axis=-1)
    return cos_t, sin_t


def kernel(x, c, ctx, c_ctx, w_mod, b_mod, ln_g, ln_b, w_in_ab, attn_sink, w_spatial, b_spatial, gm_ln_g, gm_ln_b,
           w_out_ab, w_in_pool, w_pool, pool_scale, w_out_pool, w_router, w_gate, w_up, w_down):
    b, l, d = x.shape
    depth = w_mod.shape[0]
    assert depth == 2, "layer 0 is the attention/gMLP mixer, layer 1 the pooling mixer"
    alpha = (2.0 * depth) ** 0.25
    q_w = N_Q_HEADS * HEAD_DIM
    kv_w = N_KV_HEADS * HEAD_DIM
    gm_w = d // 2

    cs = jnp.zeros((MOD_ROWS, d), F32).at[:b].set(c).at[b].set(c_ctx)
    mod_all = _mod_vectors(cs, w_mod, b_mod)
    cos_t, sin_t = _rope_tables(l)
    w_router_pad = jnp.zeros((depth, d, ROUTER_PAD), BF16).at[:, :, :N_EXPERTS].set(w_router.astype(BF16))

    def mods(i):
        m = mod_all[i, :b].reshape(b, N_MOD, 1, d)
        return [m[:, t] for t in range(N_MOD)]

    mod = mods(0)
    modc = mod_all[0, b].reshape(N_MOD, 1, 1, d)
    w_in = w_in_ab[0].astype(BF16)
    q, k, v, gu, gv = _inproj_ab(x, mod[0], mod[1], w_in, cos_t, sin_t, tm=256)
    kvx = _modulated_matmul(ctx, modc[0], modc[1], w_in[:, q_w:q_w + 2 * kv_w], tm=ctx.shape[1], per_batch_mod=False)
    bs_full = jnp.broadcast_to(b_spatial[0].T[:, :, None], (CHUNK, GM_GROUPS, gm_w // GM_GROUPS)).reshape(CHUNK, gm_w)
    ab = _attn_gmlp(attn_sink[0], q, k, v, kvx, gu, gv, w_spatial[0].astype(BF16), bs_full,
                    gm_ln_g[0].reshape(1, gm_w), gm_ln_b[0].reshape(1, gm_w))
    x1, h2, logits = _outproj_ln(ab, w_out_ab[0].astype(BF16), x, mod[2], ln_g[0, 0].reshape(1, d),
                                 ln_b[0, 0].reshape(1, d), mod[3], mod[4], w_router_pad[0], alpha=alpha, tm=256)
    idx, y = _expert_choice(h2, logits, w_gate, w_up, w_down, layer=0)
    x2 = _combine_ln(idx, y, x1, mod[5], ln_g[0, 1].reshape(1, d), ln_b[0, 1].reshape(1, d), alpha=alpha, tm=256)

    mod = mods(1)
    hp = _modulated_matmul(x2, mod[0], mod[1], w_in_pool[0].astype(BF16), tm=256, per_batch_mod=True)
    pooled = _pool_mixer(hp, w_pool[0].astype(BF16), pool_scale[0].reshape(1, -1))
    x3, h2, logits = _outproj_ln(pooled, w_out_pool[0].astype(BF16), x2, mod[2], ln_g[1, 0].reshape(1, d),
                                 ln_b[1, 0].reshape(1, d), mod[3], mod[4], w_router_pad[1], alpha=alpha, tm=256)
    idx, y = _expert_choice(h2, logits, w_gate, w_up, w_down, layer=1)
    return _combine_ln(idx, y, x3, mod[5], ln_g[1, 1].reshape(1, d), ln_b[1, 1].reshape(1, d), alpha=alpha, tm=256)
```

```python
import functools

import jax
import jax.numpy as jnp
from jax import lax
from jax.experimental import pallas as pl
from jax.experimental.pallas import tpu as pltpu

F32 = jnp.float32
BF16 = jnp.bfloat16

GRID_W = 64
HEAD_DIM = 128
N_KV_HEADS = 2
GQA_GROUP = 4
N_Q_HEADS = N_KV_HEADS * GQA_GROUP
ATTN_BLOCK = 128
AXIS_DIM = HEAD_DIM // 2
ROPE_BASE = 10000.0
GM_GROUPS = 8
CHUNK = 128
POOL_WINDOWS = (2, 4, 8, 16)
N_EXPERTS = 16
CAPACITY_FACTOR = 2
N_MOD = 6
LN_EPS = 1e-6
NEG_INF = -1e30
INV_SQRT2 = 0.7071067811865476

V7X_LANES = 128
V7X_VMEM_LIMIT_BYTES = 56 * 1024 * 1024

MOD_ROWS = 32
ROUTER_PAD = V7X_LANES


def _cparams(sem):
    return pltpu.CompilerParams(dimension_semantics=sem, vmem_limit_bytes=V7X_VMEM_LIMIT_BYTES)


def _gelu(a):
    return 0.5 * a * (1.0 + lax.erf(a * INV_SQRT2))


def _layer_norm_rows(z, g, b):
    mu = jnp.mean(z, axis=-1, keepdims=True)
    zc = z - mu
    var = jnp.mean(zc * zc, axis=-1, keepdims=True)
    return zc * lax.rsqrt(var + LN_EPS) * g + b


def _mod_kernel(cs_ref, w_ref, b_ref, o_ref):
    cs = cs_ref[...]
    s = cs * jax.nn.sigmoid(cs)
    o_ref[0] = jnp.dot(s.astype(BF16), w_ref[0].astype(BF16), preferred_element_type=F32) + b_ref[0]


def _mod_vectors(cs, w_mod, b_mod):
    depth, d, n = w_mod.shape
    tn = 512
    return pl.pallas_call(
        _mod_kernel,
        grid=(depth, n // tn),
        in_specs=[
            pl.BlockSpec((MOD_ROWS, d), lambda i, j: (0, 0)),
            pl.BlockSpec((1, d, tn), lambda i, j: (i, 0, j)),
            pl.BlockSpec((1, 1, tn), lambda i, j: (i, 0, j)),
        ],
        out_specs=pl.BlockSpec((1, MOD_ROWS, tn), lambda i, j: (i, 0, j)),
        out_shape=jax.ShapeDtypeStruct((depth, MOD_ROWS, n), F32),
        compiler_params=_cparams(("arbitrary", "arbitrary")),
        name="mod_vectors",
    )(cs, w_mod, b_mod.reshape(depth, 1, n))


def _modmm_kernel(x_ref, sh_ref, sc_ref, w_ref, o_ref, *, n_chunk):
    h = (x_ref[0] * (1.0 + sc_ref[0]) + sh_ref[0]).astype(BF16)
    n = w_ref.shape[1]
    for c in range(n // n_chunk):
        sl = slice(c * n_chunk, (c + 1) * n_chunk)
        o_ref[0, :, sl] = jnp.dot(h, w_ref[:, sl], preferred_element_type=F32).astype(o_ref.dtype)


def _modulated_matmul(x, shift, scale, w, *, tm, per_batch_mod):
    b, l, d = x.shape
    n = w.shape[1]
    mod_map = (lambda i, j: (i, 0, 0)) if per_batch_mod else (lambda i, j: (0, 0, 0))
    return pl.pallas_call(
        functools.partial(_modmm_kernel, n_chunk=min(n, 512)),
        grid=(b, l // tm),
        in_specs=[
            pl.BlockSpec((1, tm, d), lambda i, j: (i, j, 0)),
            pl.BlockSpec((1, 1, d), mod_map),
            pl.BlockSpec((1, 1, d), mod_map),
            pl.BlockSpec((d, n), lambda i, j: (0, 0)),
        ],
        out_specs=pl.BlockSpec((1, tm, n), lambda i, j: (i, j, 0)),
        out_shape=jax.ShapeDtypeStruct((b, l, n), BF16),
        compiler_params=_cparams(("arbitrary", "arbitrary")),
        name="modulated_matmul",
    )(x, shift, scale, w)


def _inproj_ab_kernel(x_ref, sh_ref, sc_ref, w_ref, cos_ref, sin_ref,
                      q_ref, k_ref, v_ref, gu_ref, gv_ref, *, q_scale):
    h = (x_ref[0] * (1.0 + sc_ref[0]) + sh_ref[0]).astype(BF16)
    tm = h.shape[0]
    cos = cos_ref[...]
    sin = sin_ref[...]
    lane = lax.broadcasted_iota(jnp.int32, (tm, HEAD_DIM), 1)
    low_half = (lane & (AXIS_DIM // 2)) == 0

    def rope(a):
        partner = jnp.where(low_half, pltpu.roll(a, HEAD_DIM - AXIS_DIM // 2, 1), pltpu.roll(a, AXIS_DIM // 2, 1))
        return a * cos + partner * sin

    nc = 512
    q_w = q_ref.shape[2]
    kv_w = k_ref.shape[2]
    gm_w = gu_ref.shape[2]
    for c in range(w_ref.shape[1] // nc):
        acc = jnp.dot(h, w_ref[:, c * nc:(c + 1) * nc], preferred_element_type=F32)
        base = c * nc
        if base < q_w:
            for j in range(nc // HEAD_DIM):
                a = acc[:, j * HEAD_DIM:(j + 1) * HEAD_DIM]
                q_ref[0, :, base + j * HEAD_DIM: base + (j + 1) * HEAD_DIM] = (rope(a) * q_scale).astype(BF16)
        elif base < q_w + 2 * kv_w:
            for j in range(kv_w // HEAD_DIM):
                a = acc[:, j * HEAD_DIM:(j + 1) * HEAD_DIM]
                k_ref[0, :, j * HEAD_DIM:(j + 1) * HEAD_DIM] = rope(a).astype(BF16)
            v_ref[0] = acc[:, kv_w:2 * kv_w].astype(BF16)
        elif base < q_w + 2 * kv_w + gm_w:
            o = base - (q_w + 2 * kv_w)
            gu_ref[0, :, o:o + nc] = _gelu(acc).astype(BF16)
        else:
            o = base - (q_w + 2 * kv_w + gm_w)
            gv_ref[0, :, o:o + nc] = _gelu(acc).astype(BF16)


def _inproj_ab(x, shift, scale, w, cos_t, sin_t, *, tm):
    b, l, d = x.shape
    q_w = N_Q_HEADS * HEAD_DIM
    kv_w = N_KV_HEADS * HEAD_DIM
    gm_w = d // 2
    n = w.shape[1]
    assert n == q_w + 2 * kv_w + 2 * gm_w and 2 * kv_w == 512
    row = lambda i, j: (i, j, 0)
    return pl.pallas_call(
        functools.partial(_inproj_ab_kernel, q_scale=HEAD_DIM ** -0.5),
        grid=(b, l // tm),
        in_specs=[
            pl.BlockSpec((1, tm, d), row),
            pl.BlockSpec((1, 1, d), lambda i, j: (i, 0, 0)),
            pl.BlockSpec((1, 1, d), lambda i, j: (i, 0, 0)),
            pl.BlockSpec((d, n), lambda i, j: (0, 0)),
            pl.BlockSpec((tm, HEAD_DIM), lambda i, j: (j, 0)),
            pl.BlockSpec((tm, HEAD_DIM), lambda i, j: (j, 0)),
        ],
        out_specs=[
            pl.BlockSpec((1, tm, q_w), row),
            pl.BlockSpec((1, tm, kv_w), row),
            pl.BlockSpec((1, tm, kv_w), row),
            pl.BlockSpec((1, tm, gm_w), row),
            pl.BlockSpec((1, tm, gm_w), row),
        ],
        out_shape=[
            jax.ShapeDtypeStruct((b, l, q_w), BF16),
            jax.ShapeDtypeStruct((b, l, kv_w), BF16),
            jax.ShapeDtypeStruct((b, l, kv_w), BF16),
            jax.ShapeDtypeStruct((b, l, gm_w), BF16),
            jax.ShapeDtypeStruct((b, l, gm_w), BF16),
        ],
        compiler_params=_cparams(("arbitrary", "arbitrary")),
        name="inproj_ab",
    )(x, shift, scale, w, cos_t, sin_t)


def _attn_gmlp_kernel(sink_ref, q_ref, kp_ref, kc_ref, kn_ref, vp_ref, vc_ref, vn_ref, kvx_ref,
                      gu_ref, gv_ref, ws_ref, bs_ref, lng_ref, lnb_ref, o_ref, *, n_blocks):
    n = pl.program_id(1)
    blk = ATTN_BLOCK
    n_ctx = kvx_ref.shape[1]
    rows = GQA_GROUP * blk
    n_keys = 3 * blk + n_ctx
    kv_w = N_KV_HEADS * HEAD_DIM

    qi = lax.broadcasted_iota(jnp.int32, (rows, n_keys), 0) & (blk - 1)
    col = lax.broadcasted_iota(jnp.int32, (rows, n_keys), 1)
    off_prev = jnp.where(n > 0, 0, 1 << 20)
    off_next = jnp.where(n < n_blocks - 1, 0, 1 << 20)
    in_next = (col >= 2 * blk) & (col < 3 * blk)
    valid = ((col >= blk) | (col >= qi + off_prev)) & (jnp.logical_not(in_next) | (col - 2 * blk + off_next <= qi))
    head_of_row = lax.broadcasted_iota(jnp.int32, (rows, 1), 0) // blk

    for hk in range(N_KV_HEADS):
        hs = slice(hk * HEAD_DIM, (hk + 1) * HEAD_DIM)
        qs = jnp.concatenate(
            [q_ref[0, :, (hk * GQA_GROUP + g) * HEAD_DIM:(hk * GQA_GROUP + g + 1) * HEAD_DIM] for g in range(GQA_GROUP)],
            axis=0)
        k_all = jnp.concatenate([kp_ref[0, :, hs], kc_ref[0, :, hs], kn_ref[0, :, hs], kvx_ref[0, :, hs]], axis=0)
        v_all = jnp.concatenate([vp_ref[0, :, hs], vc_ref[0, :, hs], vn_ref[0, :, hs],
                                 kvx_ref[0, :, kv_w + hk * HEAD_DIM: kv_w + (hk + 1) * HEAD_DIM]], axis=0)
        s = lax.dot_general(qs, k_all, (((1,), (1,)), ((), ())), preferred_element_type=F32)
        s = jnp.where(valid, s, NEG_INF)
        sink = jnp.zeros((rows, 1), F32)
        for g in range(GQA_GROUP):
            sink = jnp.where(head_of_row == g, sink_ref[hk * GQA_GROUP + g], sink)
        m = jnp.maximum(jnp.max(s, axis=-1, keepdims=True), sink)
        p = jnp.exp(s - m)
        denom = jnp.sum(p, axis=-1, keepdims=True) + jnp.exp(sink - m)
        o = jnp.dot(p.astype(BF16), v_all, preferred_element_type=F32) / denom
        for g in range(GQA_GROUP):
            h = hk * GQA_GROUP + g
            o_ref[0, :, h * HEAD_DIM:(h + 1) * HEAD_DIM] = o[g * blk:(g + 1) * blk].astype(BF16)

    att_w = N_Q_HEADS * HEAD_DIM
    vn = _layer_norm_rows(gv_ref[0].astype(F32), lng_ref[...], lnb_ref[...]).astype(BF16)
    gw = vn.shape[1] // GM_GROUPS
    for g in range(GM_GROUPS):
        gs = slice(g * gw, (g + 1) * gw)
        mixed = jnp.dot(ws_ref[g], vn[:, gs], preferred_element_type=F32) + bs_ref[:, gs]
        o_ref[0, :, att_w + g * gw: att_w + (g + 1) * gw] = (gu_ref[0, :, gs].astype(F32) * mixed).astype(BF16)


def _attn_gmlp(sink, q, k, v, kvx, gu, gv, ws, bs_full, lng, lnb):
    b, l, q_w = q.shape
    kv_w = k.shape[2]
    gm_w = gu.shape[2]
    nb = l // ATTN_BLOCK
    n_ctx = kvx.shape[1]
    cur = lambda i, j: (i, j, 0)
    prev = lambda i, j: (i, jnp.maximum(j - 1, 0), 0)
    nxt = lambda i, j: (i, jnp.minimum(j + 1, nb - 1), 0)
    const2 = lambda i, j: (0, 0)
    kv_spec = lambda m: pl.BlockSpec((1, ATTN_BLOCK, kv_w), m)
    return pl.pallas_call(
        functools.partial(_attn_gmlp_kernel, n_blocks=nb),
        grid=(b, nb),
        in_specs=[
            pl.BlockSpec(memory_space=pltpu.SMEM),
            pl.BlockSpec((1, ATTN_BLOCK, q_w), cur),
            kv_spec(prev), kv_spec(cur), kv_spec(nxt),
            kv_spec(prev), kv_spec(cur), kv_spec(nxt),
            pl.BlockSpec((1, n_ctx, 2 * kv_w), lambda i, j: (i, 0, 0)),
            pl.BlockSpec((1, ATTN_BLOCK, gm_w), cur),
            pl.BlockSpec((1, ATTN_BLOCK, gm_w), cur),
            pl.BlockSpec((GM_GROUPS, CHUNK, CHUNK), lambda i, j: (0, 0, 0)),
            pl.BlockSpec((CHUNK, gm_w), const2),
            pl.BlockSpec((1, gm_w), const2),
            pl.BlockSpec((1, gm_w), const2),
        ],
        out_specs=pl.BlockSpec((1, ATTN_BLOCK, q_w + gm_w), cur),
        out_shape=jax.ShapeDtypeStruct((b, l, q_w + gm_w), BF16),
        compiler_params=_cparams(("arbitrary", "arbitrary")),
        name="attn_gmlp",
    )(sink, q, k, k, k, v, v, v, kvx, gu, gv, ws, bs_full, lng, lnb)


def _pool_kernel(hp_ref, hc_ref, hn_ref, w_ref, sc_ref, o_ref, *, seq_len):
    n = pl.program_id(1)
    blk = hc_ref.shape[1]
    gw = w_ref.shape[1]
    t = n * blk + lax.broadcasted_iota(jnp.int32, (blk, 3 * blk), 0)
    j = (n - 1) * blk + lax.broadcasted_iota(jnp.int32, (blk, 3 * blk), 1)
    t_col = n * blk + lax.broadcasted_iota(jnp.int32, (blk, 1), 0)
    in_seq = (j >= 0) & (j < seq_len)
    for gi, win in enumerate(POOL_WINDOWS):
        gs = slice(gi * gw, (gi + 1) * gw)
        half = win // 2
        band = jnp.where(in_seq & (j >= t - half) & (j < t + half), 1.0, 0.0).astype(BF16)
        cnt = (jnp.minimum(t_col + half, seq_len) - jnp.maximum(t_col - half, 0)).astype(F32)
        x3 = jnp.concatenate([hp_ref[0, :, gs], hc_ref[0, :, gs], hn_ref[0, :, gs]], axis=0)
        mean = jnp.dot(band, x3, preferred_element_type=F32) / cnt
        dlt = (mean - hc_ref[0, :, gs].astype(F32)).astype(BF16)
        o_ref[0, :, gs] = (jnp.dot(dlt, w_ref[gi], preferred_element_type=F32) * sc_ref[:, gs]).astype(BF16)


def _pool_mixer(hp, w_pool, pool_scale):
    b, l, w = hp.shape
    blk = ATTN_BLOCK
    nb = l // blk
    cur = lambda i, j: (i, j, 0)
    prev = lambda i, j: (i, jnp.maximum(j - 1, 0), 0)
    nxt = lambda i, j: (i, jnp.minimum(j + 1, nb - 1), 0)
    return pl.pallas_call(
        functools.partial(_pool_kernel, seq_len=l),
        grid=(b, nb),
        in_specs=[
            pl.BlockSpec((1, blk, w), prev), pl.BlockSpec((1, blk, w), cur), pl.BlockSpec((1, blk, w), nxt),
            pl.BlockSpec(w_pool.shape, lambda i, j: (0, 0, 0)),
            pl.BlockSpec((1, w), lambda i, j: (0, 0)),
        ],
        out_specs=pl.BlockSpec((1, blk, w), cur),
        out_shape=jax.ShapeDtypeStruct((b, l, w), BF16),
        compiler_params=_cparams(("arbitrary", "arbitrary")),
        name="pool_mixer",
    )(hp, hp, hp, w_pool, pool_scale)


def _outproj_ln_kernel(a_ref, w_ref, x_ref, gate_ref, g_ref, b_ref, sh_ref, sc_ref, wr_ref,
                       x1_ref, h2_ref, lg_ref, *, alpha):
    y = jnp.dot(a_ref[0], w_ref[...], preferred_element_type=F32)
    x1 = _layer_norm_rows(alpha * x_ref[0] + gate_ref[0] * y, g_ref[...], b_ref[...])
    x1_ref[0] = x1
    h2 = x1 * (1.0 + sc_ref[0]) + sh_ref[0]
    h2_ref[0] = h2
    lg_ref[0] = jnp.dot(h2.astype(BF16), wr_ref[...], preferred_element_type=F32)


def _outproj_ln(a, w, x, gate, ln_g, ln_b, shift2, scale2, w_router_pad, *, alpha, tm):
    b, l, d = x.shape
    k = a.shape[2]
    row = lambda i, j: (i, j, 0)
    per_b = lambda i, j: (i, 0, 0)
    const2 = lambda i, j: (0, 0)
    return pl.pallas_call(
        functools.partial(_outproj_ln_kernel, alpha=alpha),
        grid=(b, l // tm),
        in_specs=[
            pl.BlockSpec((1, tm, k), row),
            pl.BlockSpec((k, d), const2),
            pl.BlockSpec((1, tm, d), row),
            pl.BlockSpec((1, 1, d), per_b),
            pl.BlockSpec((1, d), const2),
            pl.BlockSpec((1, d), const2),
            pl.BlockSpec((1, 1, d), per_b),
            pl.BlockSpec((1, 1, d), per_b),
            pl.BlockSpec((d, ROUTER_PAD), const2),
        ],
        out_specs=[pl.BlockSpec((1, tm, d), row), pl.BlockSpec((1, tm, d), row), pl.BlockSpec((1, tm, ROUTER_PAD), row)],
        out_shape=[
            jax.ShapeDtypeStruct((b, l, d), F32),
            jax.ShapeDtypeStruct((b, l, d), F32),
            jax.ShapeDtypeStruct((b, l, ROUTER_PAD), F32),
        ],
        compiler_params=_cparams(("arbitrary", "arbitrary")),
        name="outproj_ln",
    )(a, w, x, gate, ln_g, ln_b, shift2, scale2, w_router_pad)


def _combine_ln_kernel(idx_ref, y_ref, x_ref, gate_ref, g_ref, b_ref, o_ref, *, alpha):
    tm = x_ref.shape[1]
    tok = pl.program_id(1) * tm + lax.broadcasted_iota(jnp.int32, (tm, 1), 0)
    onehot = jnp.where(idx_ref[0] == tok, 1.0, 0.0).astype(BF16)
    moe = jnp.dot(onehot, y_ref[0], preferred_element_type=F32)
    o_ref[0] = _layer_norm_rows(alpha * x_ref[0] + gate_ref[0] * moe, g_ref[...], b_ref[...])


def _combine_ln(idx, y, x, gate, ln_g, ln_b, *, alpha, tm):
    b, l, d = x.shape
    n_rows = y.shape[1]
    row = lambda i, j: (i, j, 0)
    per_b = lambda i, j: (i, 0, 0)
    return pl.pallas_call(
        functools.partial(_combine_ln_kernel, alpha=alpha),
        grid=(b, l // tm),
        in_specs=[
            pl.BlockSpec((1, 1, n_rows), per_b),
            pl.BlockSpec((1, n_rows, d), per_b),
            pl.BlockSpec((1, tm, d), row),
            pl.BlockSpec((1, 1, d), per_b),
            pl.BlockSpec((1, d), lambda i, j: (0, 0)),
            pl.BlockSpec((1, d), lambda i, j: (0, 0)),
        ],
        out_specs=pl.BlockSpec((1, tm, d), row),
        out_shape=jax.ShapeDtypeStruct((b, l, d), F32),
        compiler_params=_cparams(("arbitrary", "arbitrary")),
        name="combine_ln",
    )(idx.reshape(b, 1, n_rows), y, x, gate, ln_g, ln_b)


def _moe_kernel(idx_ref, idx_next_ref, gates_ref, h_hbm, wg_hbm, wu_hbm, wd_hbm, y_ref,
                xs_buf, wg_buf, wu_buf, wd_buf, wg_stage, wu_stage, wd_stage, sem, wsem, *, seq_len, layer):
    e = pl.program_id(0)
    bi = pl.program_id(1)
    n_exp = pl.num_programs(0)
    n_batch = pl.num_programs(1)
    n_steps = n_exp * n_batch
    step = e * n_batch + bi
    cap = xs_buf.shape[1]
    d_rows = wg_stage.shape[0]
    f_rows = wd_stage.shape[0]

    def weight_copies(expert, chunk):
        dr = pl.ds(pl.multiple_of(chunk * d_rows, d_rows), d_rows)
        fr = pl.ds(pl.multiple_of(chunk * f_rows, f_rows), f_rows)
        return (pltpu.make_async_copy(wg_hbm.at[layer, expert, dr], wg_stage, wsem.at[0]),
                pltpu.make_async_copy(wu_hbm.at[layer, expert, dr], wu_stage, wsem.at[1]),
                pltpu.make_async_copy(wd_hbm.at[layer, expert, fr], wd_stage, wsem.at[2]))

    def convert_chunk(w_slot, chunk):
        dr = pl.ds(pl.multiple_of(chunk * d_rows, d_rows), d_rows)
        fr = pl.ds(pl.multiple_of(chunk * f_rows, f_rows), f_rows)
        wg_buf[w_slot, dr] = wg_stage[...].astype(BF16)
        wu_buf[w_slot, dr] = wu_stage[...].astype(BF16)
        wd_buf[w_slot, fr] = wd_stage[...].astype(BF16)

    def start_gather(ids_ref, batch, dst_slot):
        base = batch * seq_len
        for i in range(cap):
            pltpu.make_async_copy(h_hbm.at[pl.ds(base + ids_ref[0, 0, 0, i], 1)],
                                  xs_buf.at[dst_slot, pl.ds(i, 1)], sem.at[dst_slot]).start()

    def wait_gather(dst_slot):
        pltpu.make_async_copy(h_hbm.at[pl.ds(0, cap)], xs_buf.at[dst_slot], sem.at[dst_slot]).wait()

    @pl.when(step == 0)
    def _():
        start_gather(idx_ref, bi, 0)

        def load_chunk(chunk, carry):
            copies = weight_copies(0, chunk)
            for cp in copies:
                cp.start()
            for cp in copies:
                cp.wait()
            convert_chunk(0, chunk)
            return carry

        lax.fori_loop(0, n_batch, load_chunk, 0)

    def step_body(slot):
        next_expert = jnp.minimum(e + 1, n_exp - 1)
        for cp in weight_copies(next_expert, bi):
            cp.start()
        next_step = jnp.minimum(step + 1, n_steps - 1)
        start_gather(idx_next_ref, next_step % n_batch, 1 - slot)
        wait_gather(slot)

        w_slot = e % 2
        xs = xs_buf[slot].astype(BF16)
        a = jnp.dot(xs, wg_buf[w_slot], preferred_element_type=F32)
        u = jnp.dot(xs, wu_buf[w_slot], preferred_element_type=F32)
        act = (a * jax.nn.sigmoid(a) * u).astype(BF16)
        y_ref[0, 0] = (jnp.dot(act, wd_buf[w_slot], preferred_element_type=F32) * gates_ref[0, 0]).astype(y_ref.dtype)

        for cp in weight_copies(next_expert, bi):
            cp.wait()
        convert_chunk(1 - w_slot, bi)

        @pl.when(step == n_steps - 1)
        def _():
            wait_gather(1 - slot)

    for parity in range(2):
        pl.when(bi % 2 == parity)(functools.partial(step_body, parity))


def _moe_experts(idx, gates, h2, w_gate, w_up, w_down, *, layer):
    b, e, cap = idx.shape
    _, l, d = h2.shape
    f = w_gate.shape[3]
    assert d % b == 0 and f % b == 0, "one weight row chunk per sample step"
    assert b % 2 == 0, "gather slot = parity of the sample index"

    def next_idx_map(ei, bi):
        nxt = jnp.minimum(ei * b + bi + 1, e * b - 1)
        return (nxt % b, nxt // b, 0, 0)

    return pl.pallas_call(
        functools.partial(_moe_kernel, seq_len=l, layer=layer),
        grid=(e, b),
        in_specs=[
            pl.BlockSpec((1, 1, 1, cap), lambda ei, bi: (bi, ei, 0, 0), memory_space=pltpu.SMEM),
            pl.BlockSpec((1, 1, 1, cap), next_idx_map, memory_space=pltpu.SMEM),
            pl.BlockSpec((1, 1, cap, 1), lambda ei, bi: (bi, ei, 0, 0)),
            pl.BlockSpec(memory_space=pl.ANY),
            pl.BlockSpec(memory_space=pl.ANY),
            pl.BlockSpec(memory_space=pl.ANY),
            pl.BlockSpec(memory_space=pl.ANY),
        ],
        out_specs=pl.BlockSpec((1, 1, cap, d), lambda ei, bi: (bi, ei, 0, 0)),
        out_shape=jax.ShapeDtypeStruct((b, e, cap, d), BF16),
        scratch_shapes=[
            pltpu.VMEM((2, cap, d), F32),
            pltpu.VMEM((2, d, f), BF16), pltpu.VMEM((2, d, f), BF16), pltpu.VMEM((2, f, d), BF16),
            pltpu.VMEM((d // b, f), F32), pltpu.VMEM((d // b, f), F32), pltpu.VMEM((f // b, d), F32),
            pltpu.SemaphoreType.DMA((2,)), pltpu.SemaphoreType.DMA((3,)),
        ],
        compiler_params=_cparams(("arbitrary", "arbitrary")),
        name="moe_experts",
    )(idx.reshape(b, e, 1, cap), idx.reshape(b, e, 1, cap), gates.reshape(b, e, cap, 1), h2.reshape(b * l, d),
      w_gate, w_up, w_down)


def _expert_choice(h2, logits, w_gate, w_up, w_down, *, layer):
    b, l, d = h2.shape
    cap = (CAPACITY_FACTOR * l) // N_EXPERTS
    aff = jax.nn.softmax(logits[..., :N_EXPERTS], axis=-1)
    gates, idx = lax.top_k(jnp.swapaxes(aff, 1, 2), cap)
    idx = idx.astype(jnp.int32)
    y = _moe_experts(idx, gates, h2, w_gate, w_up, w_down, layer=layer)
    return idx.reshape(b, N_EXPERTS * cap), y.reshape(b, N_EXPERTS * cap, d)


def _rope_tables(n_tokens):
    rows = n_tokens // GRID_W
    row = jnp.repeat(jnp.arange(rows, dtype=F32), GRID_W)
    col = jnp.tile(jnp.arange(GRID_W, dtype=F32), rows)
    inv = ROPE_BASE ** (-jnp.arange(0, AXIS_DIM, 2, dtype=F32) / AXIS_DIM)
    ang_r = row[:, None] * inv
    ang_c = col[:, None] * inv
    cos_t = jnp.concatenate([jnp.cos(ang_r), jnp.cos(ang_r), jnp.cos(ang_c), jnp.cos(ang_c)], axis=-1)
    sin_t = jnp.concatenate([-jnp.sin(ang_r), jnp.sin(ang_r), -jnp.sin(ang_c), jnp.sin(ang_c)], axis=-1)
    return cos_t, sin_t


def kernel(x, c, ctx, c_ctx, w_mod, b_mod, ln_g, ln_b, w_in_ab, attn_sink, w_spatial, b_spatial, gm_ln_g, gm_ln_b,
           w_out_ab, w_in_pool, w_pool, pool_scale, w_out_pool, w_router, w_gate, w_up, w_down):
    b, l, d = x.shape
    depth = w_mod.shape[0]
    assert depth == 2, "layer 0 is the attention/gMLP mixer, layer 1 the pooling mixer"
    alpha = (2.0 * depth) ** 0.25
    q_w = N_Q_HEADS * HEAD_DIM
    kv_w = N_KV_HEADS * HEAD_DIM
    gm_w = d // 2

    cs = jnp.zeros((MOD_ROWS, d), F32).at[:b].set(c).at[b].set(c_ctx)
    mod_all = _mod_vectors(cs, w_mod, b_mod)
    cos_t, sin_t = _rope_tables(l)
    w_router_pad = jnp.zeros((depth, d, ROUTER_PAD), BF16).at[:, :, :N_EXPERTS].set(w_router.astype(BF16))

    def mods(i):
        m = mod_all[i, :b].reshape(b, N_MOD, 1, d)
        return [m[:, t] for t in range(N_MOD)]

    mod = mods(0)
    modc = mod_all[0, b].reshape(N_MOD, 1, 1, d)
    w_in = w_in_ab[0].astype(BF16)
    q, k, v, gu, gv = _inproj_ab(x, mod[0], mod[1], w_in, cos_t, sin_t, tm=256)
    kvx = _modulated_matmul(ctx, modc[0], modc[1], w_in[:, q_w:q_w + 2 * kv_w], tm=ctx.shape[1], per_batch_mod=False)
    bs_full = jnp.broadcast_to(b_spatial[0].T[:, :, None], (CHUNK, GM_GROUPS, gm_w // GM_GROUPS)).reshape(CHUNK, gm_w)
    ab = _attn_gmlp(attn_sink[0], q, k, v, kvx, gu, gv, w_spatial[0].astype(BF16), bs_full,
                    gm_ln_g[0].reshape(1, gm_w), gm_ln_b[0].reshape(1, gm_w))
    x1, h2, logits = _outproj_ln(ab, w_out_ab[0].astype(BF16), x, mod[2], ln_g[0, 0].reshape(1, d),
                                 ln_b[0, 0].reshape(1, d), mod[3], mod[4], w_router_pad[0], alpha=alpha, tm=256)
    idx, y = _expert_choice(h2, logits, w_gate, w_up, w_down, layer=0)
    x2 = _combine_ln(idx, y, x1, mod[5], ln_g[0, 1].reshape(1, d), ln_b[0, 1].reshape(1, d), alpha=alpha, tm=256)

    mod = mods(1)
    hp = _modulated_matmul(x2, mod[0], mod[1], w_in_pool[0].astype(BF16), tm=256, per_batch_mod=True)
    pooled = _pool_mixer(hp, w_pool[0].astype(BF16), pool_scale[0].reshape(1, -1))
    x3, h2, logits = _outproj_ln(pooled, w_out_pool[0].astype(BF16), x2, mod[2], ln_g[1, 0].reshape(1, d),
                                 ln_b[1, 0].reshape(1, d), mod[3], mod[4], w_router_pad[1], alpha=alpha, tm=256)
    idx, y = _expert_choice(h2, logits, w_gate, w_up, w_down, layer=1)
    return _combine_ln(idx, y, x3, mod[5], ln_g[1, 1].reshape(1, d), ln_b[1, 1].reshape(1, d), alpha=alpha, tm=256)
```

```python
import functools

import jax
import jax.numpy as jnp
from jax import lax
from jax.experimental import pallas as pl
from jax.experimental.pallas import tpu as pltpu

F32 = jnp.float32
BF16 = jnp.bfloat16

GRID_W = 64
HEAD_DIM = 128
N_KV_HEADS = 2
GQA_GROUP = 4
N_Q_HEADS = N_KV_HEADS * GQA_GROUP
ATTN_BLOCK = 128
AXIS_DIM = HEAD_DIM // 2
ROPE_BASE = 10000.0
GM_GROUPS = 8
CHUNK = 128
POOL_WINDOWS = (2, 4, 8, 16)
N_EXPERTS = 16
CAPACITY_FACTOR = 2
N_MOD = 6
LN_EPS = 1e-6
NEG_INF = -1e30
INV_SQRT2 = 0.7071067811865476
LOG2_E = 1.4426950408889634

V7X_LANES = 128
V7X_VMEM_LIMIT_BYTES = 56 * 1024 * 1024

MOD_ROWS = 32
ROUTER_PAD = V7X_LANES


def _cparams(sem):
    return pltpu.CompilerParams(dimension_semantics=sem, vmem_limit_bytes=V7X_VMEM_LIMIT_BYTES)


def _gelu(a):
    return 0.5 * a * (1.0 + lax.erf(a * INV_SQRT2))


def _layer_norm_rows(z, g, b):
    mu = jnp.mean(z, axis=-1, keepdims=True)
    zc = z - mu
    var = jnp.mean(zc * zc, axis=-1, keepdims=True)
    return zc * lax.rsqrt(var + LN_EPS) * g + b


def _mod_kernel(cs_ref, w_ref, b_ref, o_ref):
    cs = cs_ref[...]
    s = cs * jax.nn.sigmoid(cs)
    o_ref[0] = jnp.dot(s.astype(BF16), w_ref[0].astype(BF16), preferred_element_type=F32) + b_ref[0]


def _mod_vectors(cs, w_mod, b_mod):
    depth, d, n = w_mod.shape
    tn = 512
    return pl.pallas_call(
        _mod_kernel,
        grid=(depth, n // tn),
        in_specs=[
            pl.BlockSpec((MOD_ROWS, d), lambda i, j: (0, 0)),
            pl.BlockSpec((1, d, tn), lambda i, j: (i, 0, j)),
            pl.BlockSpec((1, 1, tn), lambda i, j: (i, 0, j)),
        ],
        out_specs=pl.BlockSpec((1, MOD_ROWS, tn), lambda i, j: (i, 0, j)),
        out_shape=jax.ShapeDtypeStruct((depth, MOD_ROWS, n), F32),
        compiler_params=_cparams(("arbitrary", "arbitrary")),
        name="mod_vectors",
    )(cs, w_mod, b_mod.reshape(depth, 1, n))


def _modmm_kernel(x_ref, sh_ref, sc_ref, w_ref, o_ref, *, n_chunk):
    h = (x_ref[0] * (1.0 + sc_ref[0]) + sh_ref[0]).astype(BF16)
    n = w_ref.shape[1]
    for c in range(n // n_chunk):
        sl = slice(c * n_chunk, (c + 1) * n_chunk)
        o_ref[0, :, sl] = jnp.dot(h, w_ref[:, sl], preferred_element_type=F32).astype(o_ref.dtype)


def _modulated_matmul(x, shift, scale, w, *, tm, per_batch_mod):
    b, l, d = x.shape
    n = w.shape[1]
    mod_map = (lambda i, j: (i, 0, 0)) if per_batch_mod else (lambda i, j: (0, 0, 0))
    return pl.pallas_call(
        functools.partial(_modmm_kernel, n_chunk=min(n, 512)),
        grid=(b, l // tm),
        in_specs=[
            pl.BlockSpec((1, tm, d), lambda i, j: (i, j, 0)),
            pl.BlockSpec((1, 1, d), mod_map),
            pl.BlockSpec((1, 1, d), mod_map),
            pl.BlockSpec((d, n), lambda i, j: (0, 0)),
        ],
        out_specs=pl.BlockSpec((1, tm, n), lambda i, j: (i, j, 0)),
        out_shape=jax.ShapeDtypeStruct((b, l, n), BF16),
        compiler_params=_cparams(("arbitrary", "arbitrary")),
        name="modulated_matmul",
    )(x, shift, scale, w)


def _inproj_ab_kernel(x_ref, sh_ref, sc_ref, w_ref, cos_ref, sin_ref,
                      q_ref, k_ref, v_ref, gu_ref, gv_ref, *, q_scale):
    h = (x_ref[0] * (1.0 + sc_ref[0]) + sh_ref[0]).astype(BF16)
    tm = h.shape[0]
    cos = cos_ref[...]
    sin = sin_ref[...]
    lane = lax.broadcasted_iota(jnp.int32, (tm, HEAD_DIM), 1)
    low_half = (lane & (AXIS_DIM // 2)) == 0

    def rope(a):
        partner = jnp.where(low_half, pltpu.roll(a, HEAD_DIM - AXIS_DIM // 2, 1), pltpu.roll(a, AXIS_DIM // 2, 1))
        return a * cos + partner * sin

    nc = 512
    q_w = q_ref.shape[2]
    kv_w = k_ref.shape[2]
    gm_w = gu_ref.shape[2]
    for c in range(w_ref.shape[1] // nc):
        acc = jnp.dot(h, w_ref[:, c * nc:(c + 1) * nc], preferred_element_type=F32)
        base = c * nc
        if base < q_w:
            for j in range(nc // HEAD_DIM):
                a = acc[:, j * HEAD_DIM:(j + 1) * HEAD_DIM]
                q_ref[0, :, base + j * HEAD_DIM: base + (j + 1) * HEAD_DIM] = (rope(a) * q_scale).astype(BF16)
        elif base < q_w + 2 * kv_w:
            for j in range(kv_w // HEAD_DIM):
                a = acc[:, j * HEAD_DIM:(j + 1) * HEAD_DIM]
                k_ref[0, :, j * HEAD_DIM:(j + 1) * HEAD_DIM] = rope(a).astype(BF16)
            v_ref[0] = acc[:, kv_w:2 * kv_w].astype(BF16)
        elif base < q_w + 2 * kv_w + gm_w:
            o = base - (q_w + 2 * kv_w)
            gu_ref[0, :, o:o + nc] = _gelu(acc).astype(BF16)
        else:
            o = base - (q_w + 2 * kv_w + gm_w)
            gv_ref[0, :, o:o + nc] = _gelu(acc).astype(BF16)


def _inproj_ab(x, shift, scale, w, cos_t, sin_t, *, tm):
    b, l, d = x.shape
    q_w = N_Q_HEADS * HEAD_DIM
    kv_w = N_KV_HEADS * HEAD_DIM
    gm_w = d // 2
    n = w.shape[1]
    assert n == q_w + 2 * kv_w + 2 * gm_w and 2 * kv_w == 512
    row = lambda i, j: (i, j, 0)
    return pl.pallas_call(
        functools.partial(_inproj_ab_kernel, q_scale=HEAD_DIM ** -0.5 * LOG2_E),
        grid=(b, l // tm),
        in_specs=[
            pl.BlockSpec((1, tm, d), row),
            pl.BlockSpec((1, 1, d), lambda i, j: (i, 0, 0)),
            pl.BlockSpec((1, 1, d), lambda i, j: (i, 0, 0)),
            pl.BlockSpec((d, n), lambda i, j: (0, 0)),
            pl.BlockSpec((tm, HEAD_DIM), lambda i, j: (j, 0)),
            pl.BlockSpec((tm, HEAD_DIM), lambda i, j: (j, 0)),
        ],
        out_specs=[
            pl.BlockSpec((1, tm, q_w), row),
            pl.BlockSpec((1, tm, kv_w), row),
            pl.BlockSpec((1, tm, kv_w), row),
            pl.BlockSpec((1, tm, gm_w), row),
            pl.BlockSpec((1, tm, gm_w), row),
        ],
        out_shape=[
            jax.ShapeDtypeStruct((b, l, q_w), BF16),
            jax.ShapeDtypeStruct((b, l, kv_w), BF16),
            jax.ShapeDtypeStruct((b, l, kv_w), BF16),
            jax.ShapeDtypeStruct((b, l, gm_w), BF16),
            jax.ShapeDtypeStruct((b, l, gm_w), BF16),
        ],
        compiler_params=_cparams(("arbitrary", "arbitrary")),
        name="inproj_ab",
    )(x, shift, scale, w, cos_t, sin_t)


def _attn_gmlp_kernel(sink_ref, q_ref, kp_ref, kc_ref, kn_ref, vp_ref, vc_ref, vn_ref, kvx_ref,
                      gu_ref, gv_ref, ws_ref, bs_ref, lng_ref, lnb_ref, o_ref, *, n_blocks):
    n = pl.program_id(1)
    blk = ATTN_BLOCK
    n_ctx = kvx_ref.shape[1]
    rows = GQA_GROUP * blk
    n_keys = 3 * blk + n_ctx
    kv_w = N_KV_HEADS * HEAD_DIM

    qi = lax.broadcasted_iota(jnp.int32, (rows, blk), 0) & (blk - 1)
    kj = lax.broadcasted_iota(jnp.int32, (rows, blk), 1)
    off_prev = jnp.where(n > 0, 0, 1 << 20)
    off_next = jnp.where(n < n_blocks - 1, 0, 1 << 20)
    valid_prev = kj >= qi + off_prev
    valid_next = kj + off_next <= qi
    head_of_row = lax.broadcasted_iota(jnp.int32, (rows, 1), 0) // blk

    for hk in range(N_KV_HEADS):
        hs = slice(hk * HEAD_DIM, (hk + 1) * HEAD_DIM)
        qs = jnp.concatenate(
            [q_ref[0, :, (hk * GQA_GROUP + g) * HEAD_DIM:(hk * GQA_GROUP + g + 1) * HEAD_DIM] for g in range(GQA_GROUP)],
            axis=0)
        k_all = jnp.concatenate([kp_ref[0, :, hs], kc_ref[0, :, hs], kn_ref[0, :, hs], kvx_ref[0, :, hs]], axis=0)
        v_all = jnp.concatenate([vp_ref[0, :, hs], vc_ref[0, :, hs], vn_ref[0, :, hs],
                                 kvx_ref[0, :, kv_w + hk * HEAD_DIM: kv_w + (hk + 1) * HEAD_DIM]], axis=0)
        s = lax.dot_general(qs, k_all, (((1,), (1,)), ((), ())), preferred_element_type=F32)
        s = jnp.concatenate([jnp.where(valid_prev, s[:, :blk], NEG_INF), s[:, blk:2 * blk],
                             jnp.where(valid_next, s[:, 2 * blk:3 * blk], NEG_INF), s[:, 3 * blk:]], axis=1)
        sink = jnp.zeros((rows, 1), F32)
        for g in range(GQA_GROUP):
            sink = jnp.where(head_of_row == g, sink_ref[hk * GQA_GROUP + g] * LOG2_E, sink)
        m = jnp.maximum(jnp.max(s, axis=-1, keepdims=True), sink)
        p = jnp.exp2(s - m)
        denom = jnp.sum(p, axis=-1, keepdims=True) + jnp.exp2(sink - m)
        o = jnp.dot(p.astype(BF16), v_all, preferred_element_type=F32) / denom
        for g in range(GQA_GROUP):
            h = hk * GQA_GROUP + g
            o_ref[0, :, h * HEAD_DIM:(h + 1) * HEAD_DIM] = o[g * blk:(g + 1) * blk].astype(BF16)

    att_w = N_Q_HEADS * HEAD_DIM
    vn = _layer_norm_rows(gv_ref[0].astype(F32), lng_ref[...], lnb_ref[...]).astype(BF16)
    gw = vn.shape[1] // GM_GROUPS
    for g in range(GM_GROUPS):
        gs = slice(g * gw, (g + 1) * gw)
        mixed = jnp.dot(ws_ref[g], vn[:, gs], preferred_element_type=F32) + bs_ref[:, gs]
        o_ref[0, :, att_w + g * gw: att_w + (g + 1) * gw] = (gu_ref[0, :, gs].astype(F32) * mixed).astype(BF16)


def _attn_gmlp(sink, q, k, v, kvx, gu, gv, ws, bs_full, lng, lnb):
    b, l, q_w = q.shape
    kv_w = k.shape[2]
    gm_w = gu.shape[2]
    nb = l // ATTN_BLOCK
    n_ctx = kvx.shape[1]
    cur = lambda i, j: (i, j, 0)
    prev = lambda i, j: (i, jnp.maximum(j - 1, 0), 0)
    nxt = lambda i, j: (i, jnp.minimum(j + 1, nb - 1), 0)
    const2 = lambda i, j: (0, 0)
    kv_spec = lambda m: pl.BlockSpec((1, ATTN_BLOCK, kv_w), m)
    return pl.pallas_call(
        functools.partial(_attn_gmlp_kernel, n_blocks=nb),
        grid=(b, nb),
        in_specs=[
            pl.BlockSpec(memory_space=pltpu.SMEM),
            pl.BlockSpec((1, ATTN_BLOCK, q_w), cur),
            kv_spec(prev), kv_spec(cur), kv_spec(nxt),
            kv_spec(prev), kv_spec(cur), kv_spec(nxt),
            pl.BlockSpec((1, n_ctx, 2 * kv_w), lambda i, j: (i, 0, 0)),
            pl.BlockSpec((1, ATTN_BLOCK, gm_w), cur),
            pl.BlockSpec((1, ATTN_BLOCK, gm_w), cur),
            pl.BlockSpec((GM_GROUPS, CHUNK, CHUNK), lambda i, j: (0, 0, 0)),
            pl.BlockSpec((CHUNK, gm_w), const2),
            pl.BlockSpec((1, gm_w), const2),
            pl.BlockSpec((1, gm_w), const2),
        ],
        out_specs=pl.BlockSpec((1, ATTN_BLOCK, q_w + gm_w), cur),
        out_shape=jax.ShapeDtypeStruct((b, l, q_w + gm_w), BF16),
        compiler_params=_cparams(("arbitrary", "arbitrary")),
        name="attn_gmlp",
    )(sink, q, k, k, k, v, v, v, kvx, gu, gv, ws, bs_full, lng, lnb)


def _pool_kernel(hp_ref, hc_ref, hn_ref, w_ref, sc_ref, o_ref, *, seq_len):
    n = pl.program_id(1)
    blk = hc_ref.shape[1]
    gw = w_ref.shape[1]
    t = n * blk + lax.broadcasted_iota(jnp.int32, (blk, 3 * blk), 0)
    j = (n - 1) * blk + lax.broadcasted_iota(jnp.int32, (blk, 3 * blk), 1)
    t_col = n * blk + lax.broadcasted_iota(jnp.int32, (blk, 1), 0)
    in_seq = (j >= 0) & (j < seq_len)
    for gi, win in enumerate(POOL_WINDOWS):
        gs = slice(gi * gw, (gi + 1) * gw)
        half = win // 2
        band = jnp.where(in_seq & (j >= t - half) & (j < t + half), 1.0, 0.0).astype(BF16)
        cnt = (jnp.minimum(t_col + half, seq_len) - jnp.maximum(t_col - half, 0)).astype(F32)
        x3 = jnp.concatenate([hp_ref[0, :, gs], hc_ref[0, :, gs], hn_ref[0, :, gs]], axis=0)
        mean = jnp.dot(band, x3, preferred_element_type=F32) / cnt
        dlt = (mean - hc_ref[0, :, gs].astype(F32)).astype(BF16)
        o_ref[0, :, gs] = (jnp.dot(dlt, w_ref[gi], preferred_element_type=F32) * sc_ref[:, gs]).astype(BF16)


def _pool_mixer(hp, w_pool, pool_scale):
    b, l, w = hp.shape
    blk = ATTN_BLOCK
    nb = l // blk
    cur = lambda i, j: (i, j, 0)
    prev = lambda i, j: (i, jnp.maximum(j - 1, 0), 0)
    nxt = lambda i, j: (i, jnp.minimum(j + 1, nb - 1), 0)
    return pl.pallas_call(
        functools.partial(_pool_kernel, seq_len=l),
        grid=(b, nb),
        in_specs=[
            pl.BlockSpec((1, blk, w), prev), pl.BlockSpec((1, blk, w), cur), pl.BlockSpec((1, blk, w), nxt),
            pl.BlockSpec(w_pool.shape, lambda i, j: (0, 0, 0)),
            pl.BlockSpec((1, w), lambda i, j: (0, 0)),
        ],
        out_specs=pl.BlockSpec((1, blk, w), cur),
        out_shape=jax.ShapeDtypeStruct((b, l, w), BF16),
        compiler_params=_cparams(("arbitrary", "arbitrary")),
        name="pool_mixer",
    )(hp, hp, hp, w_pool, pool_scale)


def _outproj_ln_kernel(a_ref, w_ref, x_ref, gate_ref, g_ref, b_ref, sh_ref, sc_ref, wr_ref,
                       x1_ref, h2_ref, lg_ref, *, alpha):
    y = jnp.dot(a_ref[0], w_ref[...], preferred_element_type=F32)
    x1 = _layer_norm_rows(alpha * x_ref[0] + gate_ref[0] * y, g_ref[...], b_ref[...])
    x1_ref[0] = x1
    h2 = x1 * (1.0 + sc_ref[0]) + sh_ref[0]
    h2_ref[0] = h2
    lg_ref[0] = jnp.dot(h2.astype(BF16), wr_ref[...], preferred_element_type=F32)


def _outproj_ln(a, w, x, gate, ln_g, ln_b, shift2, scale2, w_router_pad, *, alpha, tm):
    b, l, d = x.shape
    k = a.shape[2]
    row = lambda i, j: (i, j, 0)
    per_b = lambda i, j: (i, 0, 0)
    const2 = lambda i, j: (0, 0)
    return pl.pallas_call(
        functools.partial(_outproj_ln_kernel, alpha=alpha),
        grid=(b, l // tm),
        in_specs=[
            pl.BlockSpec((1, tm, k), row),
            pl.BlockSpec((k, d), const2),
            pl.BlockSpec((1, tm, d), row),
            pl.BlockSpec((1, 1, d), per_b),
            pl.BlockSpec((1, d), const2),
            pl.BlockSpec((1, d), const2),
            pl.BlockSpec((1, 1, d), per_b),
            pl.BlockSpec((1, 1, d), per_b),
            pl.BlockSpec((d, ROUTER_PAD), const2),
        ],
        out_specs=[pl.BlockSpec((1, tm, d), row), pl.BlockSpec((1, tm, d), row), pl.BlockSpec((1, tm, ROUTER_PAD), row)],
        out_shape=[
            jax.ShapeDtypeStruct((b, l, d), F32),
            jax.ShapeDtypeStruct((b, l, d), F32),
            jax.ShapeDtypeStruct((b, l, ROUTER_PAD), F32),
        ],
        compiler_params=_cparams(("arbitrary", "arbitrary")),
        name="outproj_ln",
    )(a, w, x, gate, ln_g, ln_b, shift2, scale2, w_router_pad)


def _combine_ln_kernel(idx_ref, y_ref, x_ref, gate_ref, g_ref, b_ref, o_ref, *, alpha):
    tm = x_ref.shape[1]
    tok = pl.program_id(1) * tm + lax.broadcasted_iota(jnp.int32, (tm, 1), 0)
    onehot = jnp.where(idx_ref[0] == tok, 1.0, 0.0).astype(BF16)
    moe = jnp.dot(onehot, y_ref[0], preferred_element_type=F32)
    o_ref[0] = _layer_norm_rows(alpha * x_ref[0] + gate_ref[0] * moe, g_ref[...], b_ref[...])


def _combine_ln(idx, y, x, gate, ln_g, ln_b, *, alpha, tm):
    b, l, d = x.shape
    n_rows = y.shape[1]
    row = lambda i, j: (i, j, 0)
    per_b = lambda i, j: (i, 0, 0)
    return pl.pallas_call(
        functools.partial(_combine_ln_kernel, alpha=alpha),
        grid=(b, l // tm),
        in_specs=[
            pl.BlockSpec((1, 1, n_rows), per_b),
            pl.BlockSpec((1, n_rows, d), per_b),
            pl.BlockSpec((1, tm, d), row),
            pl.BlockSpec((1, 1, d), per_b),
            pl.BlockSpec((1, d), lambda i, j: (0, 0)),
            pl.BlockSpec((1, d), lambda i, j: (0, 0)),
        ],
        out_specs=pl.BlockSpec((1, tm, d), row),
        out_shape=jax.ShapeDtypeStruct((b, l, d), F32),
        compiler_params=_cparams(("arbitrary", "arbitrary")),
        name="combine_ln",
    )(idx.reshape(b, 1, n_rows), y, x, gate, ln_g, ln_b)


def _moe_kernel(idx_ref, idx_next_ref, gates_ref, h_hbm, wg_hbm, wu_hbm, wd_hbm, y_ref,
                xs_buf, wg_buf, wu_buf, wd_buf, wg_stage, wu_stage, wd_stage, sem, wsem, *, seq_len, layer):
    e = pl.program_id(0)
    bi = pl.program_id(1)
    n_exp = pl.num_programs(0)
    n_batch = pl.num_programs(1)
    n_steps = n_exp * n_batch
    step = e * n_batch + bi
    cap = xs_buf.shape[1]
    d_rows = wg_stage.shape[0]
    f_rows = wd_stage.shape[0]

    def weight_copies(expert, chunk):
        dr = pl.ds(pl.multiple_of(chunk * d_rows, d_rows), d_rows)
        fr = pl.ds(pl.multiple_of(chunk * f_rows, f_rows), f_rows)
        return (pltpu.make_async_copy(wg_hbm.at[layer, expert, dr], wg_stage, wsem.at[0]),
                pltpu.make_async_copy(wu_hbm.at[layer, expert, dr], wu_stage, wsem.at[1]),
                pltpu.make_async_copy(wd_hbm.at[layer, expert, fr], wd_stage, wsem.at[2]))

    def convert_chunk(w_slot, chunk):
        dr = pl.ds(pl.multiple_of(chunk * d_rows, d_rows), d_rows)
        fr = pl.ds(pl.multiple_of(chunk * f_rows, f_rows), f_rows)
        wg_buf[w_slot, dr] = wg_stage[...].astype(BF16)
        wu_buf[w_slot, dr] = wu_stage[...].astype(BF16)
        wd_buf[w_slot, fr] = wd_stage[...].astype(BF16)

    def start_gather(ids_ref, batch, dst_slot):
        base = batch * seq_len
        for i in range(cap):
            pltpu.make_async_copy(h_hbm.at[pl.ds(base + ids_ref[0, 0, 0, i], 1)],
                                  xs_buf.at[dst_slot, pl.ds(i, 1)], sem.at[dst_slot]).start()

    def wait_gather(dst_slot):
        pltpu.make_async_copy(h_hbm.at[pl.ds(0, cap)], xs_buf.at[dst_slot], sem.at[dst_slot]).wait()

    @pl.when(step == 0)
    def _():
        start_gather(idx_ref, bi, 0)

        def load_chunk(chunk, carry):
            copies = weight_copies(0, chunk)
            for cp in copies:
                cp.start()
            for cp in copies:
                cp.wait()
            convert_chunk(0, chunk)
            return carry

        lax.fori_loop(0, n_batch, load_chunk, 0)

    def step_body(slot):
        next_expert = jnp.minimum(e + 1, n_exp - 1)
        for cp in weight_copies(next_expert, bi):
            cp.start()
        next_step = jnp.minimum(step + 1, n_steps - 1)
        start_gather(idx_next_ref, next_step % n_batch, 1 - slot)
        wait_gather(slot)

        w_slot = e % 2
        xs = xs_buf[slot].astype(BF16)
        a = jnp.dot(xs, wg_buf[w_slot], preferred_element_type=F32)
        u = jnp.dot(xs, wu_buf[w_slot], preferred_element_type=F32)
        act = (a * jax.nn.sigmoid(a) * u).astype(BF16)
        y_ref[0, 0] = (jnp.dot(act, wd_buf[w_slot], preferred_element_type=F32) * gates_ref[0, 0]).astype(y_ref.dtype)

        for cp in weight_copies(next_expert, bi):
            cp.wait()
        convert_chunk(1 - w_slot, bi)

        @pl.when(step == n_steps - 1)
        def _():
            wait_gather(1 - slot)

    for parity in range(2):
        pl.when(bi % 2 == parity)(functools.partial(step_body, parity))


def _moe_experts(idx, gates, h2, w_gate, w_up, w_down, *, layer):
    b, e, cap = idx.shape
    _, l, d = h2.shape
    f = w_gate.shape[3]
    assert d % b == 0 and f % b == 0, "one weight row chunk per sample step"
    assert b % 2 == 0, "gather slot = parity of the sample index"

    def next_idx_map(ei, bi):
        nxt = jnp.minimum(ei * b + bi + 1, e * b - 1)
        return (nxt % b, nxt // b, 0, 0)

    return pl.pallas_call(
        functools.partial(_moe_kernel, seq_len=l, layer=layer),
        grid=(e, b),
        in_specs=[
            pl.BlockSpec((1, 1, 1, cap), lambda ei, bi: (bi, ei, 0, 0), memory_space=pltpu.SMEM),
            pl.BlockSpec((1, 1, 1, cap), next_idx_map, memory_space=pltpu.SMEM),
            pl.BlockSpec((1, 1, cap, 1), lambda ei, bi: (bi, ei, 0, 0)),
            pl.BlockSpec(memory_space=pl.ANY),
            pl.BlockSpec(memory_space=pl.ANY),
            pl.BlockSpec(memory_space=pl.ANY),
            pl.BlockSpec(memory_space=pl.ANY),
        ],
        out_specs=pl.BlockSpec((1, 1, cap, d), lambda ei, bi: (bi, ei, 0, 0)),
        out_shape=jax.ShapeDtypeStruct((b, e, cap, d), BF16),
        scratch_shapes=[
            pltpu.VMEM((2, cap, d), F32),
            pltpu.VMEM((2, d, f), BF16), pltpu.VMEM((2, d, f), BF16), pltpu.VMEM((2, f, d), BF16),
            pltpu.VMEM((d // b, f), F32), pltpu.VMEM((d // b, f), F32), pltpu.VMEM((f // b, d), F32),
            pltpu.SemaphoreType.DMA((2,)), pltpu.SemaphoreType.DMA((3,)),
        ],
        compiler_params=_cparams(("arbitrary", "arbitrary")),
        name="moe_experts",
    )(idx.reshape(b, e, 1, cap), idx.reshape(b, e, 1, cap), gates.reshape(b, e, cap, 1), h2.reshape(b * l, d),
      w_gate, w_up, w_down)


def _expert_choice(h2, logits, w_gate, w_up, w_down, *, layer):
    b, l, d = h2.shape
    cap = (CAPACITY_FACTOR * l) // N_EXPERTS
    aff = jax.nn.softmax(logits[..., :N_EXPERTS], axis=-1)
    gates, idx = lax.top_k(jnp.swapaxes(aff, 1, 2), cap)
    idx = idx.astype(jnp.int32)
    y = _moe_experts(idx, gates, h2, w_gate, w_up, w_down, layer=layer)
    return idx.reshape(b, N_EXPERTS * cap), y.reshape(b, N_EXPERTS * cap, d)


def _rope_tables(n_tokens):
    rows = n_tokens // GRID_W
    row = jnp.repeat(jnp.arange(rows, dtype=F32), GRID_W)
    col = jnp.tile(jnp.arange(GRID_W, dtype=F32), rows)
    inv = ROPE_BASE ** (-jnp.arange(0, AXIS_DIM, 2, dtype=F32) / AXIS_DIM)
    ang_r = row[:, None] * inv
    ang_c = col[:, None] * inv
    cos_t = jnp.concatenate([jnp.cos(ang_r), jnp.cos(ang_r), jnp.cos(ang_c), jnp.cos(ang_c)], axis=-1)
    sin_t = jnp.concatenate([-jnp.sin(ang_r), jnp.sin(ang_r), -jnp.sin(ang_c), jnp.sin(ang_c)], axis=-1)
    return cos_t, sin_t


def kernel(x, c, ctx, c_ctx, w_mod, b_mod, ln_g, ln_b, w_in_ab, attn_sink, w_spatial, b_spatial, gm_ln_g, gm_ln_b,
           w_out_ab, w_in_pool, w_pool, pool_scale, w_out_pool, w_router, w_gate, w_up, w_down):
    b, l, d = x.shape
    depth = w_mod.shape[0]
    assert depth == 2, "layer 0 is the attention/gMLP mixer, layer 1 the pooling mixer"
    alpha = (2.0 * depth) ** 0.25
    q_w = N_Q_HEADS * HEAD_DIM
    kv_w = N_KV_HEADS * HEAD_DIM
    gm_w = d // 2

    cs = jnp.zeros((MOD_ROWS, d), F32).at[:b].set(c).at[b].set(c_ctx)
    mod_all = _mod_vectors(cs, w_mod, b_mod)
    cos_t, sin_t = _rope_tables(l)
    w_router_pad = jnp.zeros((depth, d, ROUTER_PAD), BF16).at[:, :, :N_EXPERTS].set(w_router.astype(BF16))

    def mods(i):
        m = mod_all[i, :b].reshape(b, N_MOD, 1, d)
        return [m[:, t] for t in range(N_MOD)]

    mod = mods(0)
    modc = mod_all[0, b].reshape(N_MOD, 1, 1, d)
    w_in = w_in_ab[0].astype(BF16)
    q, k, v, gu, gv = _inproj_ab(x, mod[0], mod[1], w_in, cos_t, sin_t, tm=256)
    kvx = _modulated_matmul(ctx, modc[0], modc[1], w_in[:, q_w:q_w + 2 * kv_w], tm=ctx.shape[1], per_batch_mod=False)
    bs_full = jnp.broadcast_to(b_spatial[0].T[:, :, None], (CHUNK, GM_GROUPS, gm_w // GM_GROUPS)).reshape(CHUNK, gm_w)
    ab = _attn_gmlp(attn_sink[0], q, k, v, kvx, gu, gv, w_spatial[0].astype(BF16), bs_full,
                    gm_ln_g[0].reshape(1, gm_w), gm_ln_b[0].reshape(1, gm_w))
    x1, h2, logits = _outproj_ln(ab, w_out_ab[0].astype(BF16), x, mod[2], ln_g[0, 0].reshape(1, d),
                                 ln_b[0, 0].reshape(1, d), mod[3], mod[4], w_router_pad[0], alpha=alpha, tm=256)
    idx, y = _expert_choice(h2, logits, w_gate, w_up, w_down, layer=0)
    x2 = _combine_ln(idx, y, x1, mod[5], ln_g[0, 1].reshape(1, d), ln_b[0, 1].reshape(1, d), alpha=alpha, tm=256)

    mod = mods(1)
    hp = _modulated_matmul(x2, mod[0], mod[1], w_in_pool[0].astype(BF16), tm=256, per_batch_mod=True)
    pooled = _pool_mixer(hp, w_pool[0].astype(BF16), pool_scale[0].reshape(1, -1))
    x3, h2, logits = _outproj_ln(pooled, w_out_pool[0].astype(BF16), x2, mod[2], ln_g[1, 0].reshape(1, d),
                                 ln_b[1, 0].reshape(1, d), mod[3], mod[4], w_router_pad[1], alpha=alpha, tm=256)
    idx, y = _expert_choice(h2, logits, w_gate, w_up, w_down, layer=1)
    return _combine_ln(idx, y, x3, mod[5], ln_g[1, 1].reshape(1, d), ln_b[1, 1].reshape(1, d), alpha=alpha, tm=256)
```
